```python
import math
import jax, jax.numpy as jnp
from jax import lax
import numpy as np

D_MODEL = 2048
BATCH = 2
SEQ = 8192
DEPTH = 4

GRID_W = 64
CTX_LEN = 256
EPS = 1e-6
N_SUB = 3
N_MOD = 3 * N_SUB
D_FF = 5632
S5_W = D_MODEL // 2
S5_H = 16
S5_G = S5_W // S5_H
S5_P = 64
S5_DT_MIN = 1e-3
S5_DT_MAX = 1e-1
CONV_W = D_MODEL // 2
CONV_K = 3
MIX_IN = S5_W + 3 * CONV_W
MIX_OUT = S5_W + CONV_W
ATTN_DH = 64
ATTN_H = D_MODEL // (2 * ATTN_DH)
ATTN_VD = 2 * ATTN_DH
ROPE_PAIRS_AXIS = ATTN_DH // 4
ROPE_BASE = 10000.0
Q_BLOCK = 128
N_EVEN = (DEPTH + 1) // 2
N_ODD = DEPTH // 2

kernel_name = 'hybrid_s5_conv_diffattn_dit_block'

F32 = jnp.float32


def rms_norm(x, g):
    xf = x.astype(F32)
    y = xf * lax.rsqrt(jnp.mean(xf * xf, axis=-1, keepdims=True) + EPS)
    return (y * g.astype(F32)).astype(x.dtype)


def ada_params(cvec, w, b):
    m = jnp.dot(jax.nn.silu(cvec), w) + b
    return m.reshape(m.shape[:-1] + (N_MOD, D_MODEL))


def mod_in(x, g, m, s):
    return rms_norm(x, g) * (1.0 + m[:, :, 3 * s + 1]) + m[:, :, 3 * s]


def mod_out(x, y, g, m, s, weight):
    return x + weight * m[:, :, 3 * s + 2] * rms_norm(y, g)


def ffn_sublayer(x, m, s, g_pre, g_post, wg, wu, wd):
    h = mod_in(x, g_pre, m, s)
    y = jnp.dot(jax.nn.silu(jnp.dot(h, wg)) * jnp.dot(h, wu), wd)
    return mod_out(x, y, g_post, m, s, 0.5)


def s5_discretize(lam_re, lam_im, log_step, b_re, b_im):
    lam_re, lam_im = lam_re.astype(F32), lam_im.astype(F32)
    dt = jnp.exp(log_step.astype(F32))[:, None]
    mag = jnp.exp(lam_re * dt)
    a_re, a_im = mag * jnp.cos(lam_im * dt), mag * jnp.sin(lam_im * dt)
    inv = 1.0 / (lam_re * lam_re + lam_im * lam_im)
    co_re = ((a_re - 1.0) * lam_re + a_im * lam_im) * inv
    co_im = (a_im * lam_re - (a_re - 1.0) * lam_im) * inv
    b_re, b_im = b_re.astype(F32), b_im.astype(F32)
    bb_re = co_re[..., None] * b_re - co_im[..., None] * b_im
    bb_im = co_re[..., None] * b_im + co_im[..., None] * b_re
    return a_re, a_im, bb_re, bb_im


def _linrec_combine(e1, e2):
    a1r, a1i, b1r, b1i = e1
    a2r, a2i, b2r, b2i = e2
    return (a1r * a2r - a1i * a2i, a1r * a2i + a1i * a2r,
            a2r * b1r - a2i * b1i + b2r, a2r * b1i + a2i * b1r + b2i)


def s5_scan(a_re, a_im, bu_re, bu_im, h0_re, h0_im, reverse):
    first = bu_re.shape[1] - 1 if reverse else 0
    bu_re = bu_re.at[:, first].add(a_re * h0_re - a_im * h0_im)
    bu_im = bu_im.at[:, first].add(a_re * h0_im + a_im * h0_re)
    shape = (1,) + bu_re.shape[1:]
    ar = jnp.broadcast_to(a_re, shape)
    ai = jnp.broadcast_to(a_im, shape)
    _, _, h_re, h_im = lax.associative_scan(_linrec_combine, (ar, ai, bu_re, bu_im), reverse=reverse, axis=1)
    return h_re, h_im


def s5_readout(y, u, d, w_glu, b_glu):
    z = jax.nn.gelu(y.reshape(u.shape) + d.astype(F32) * u.astype(F32))
    return (z * jax.nn.sigmoid(jnp.dot(z, w_glu.astype(F32)) + b_glu.astype(F32))).astype(u.dtype)


def s5_mix(u_c, u_l, lam_re, lam_im, log_step, b_re, b_im, c_re, c_im, d, w_glu, b_glu, need_ctx):
    def groups(u):
        return u.astype(F32).reshape(u.shape[:2] + (S5_G, S5_H))
    uc, ul = groups(u_c), groups(u_l)
    zeros = jnp.zeros((uc.shape[0], S5_G, S5_P), F32)
    ys_c, ys_l = [], []
    for direction, reverse in ((0, False), (1, True)):
        a_re, a_im, bb_re, bb_im = s5_discretize(lam_re[direction], lam_im[direction], log_step[direction],
                                                 b_re[direction], b_im[direction])
        cr, ci = c_re[direction].astype(F32), c_im[direction].astype(F32)

        def drive(u):
            return jnp.einsum('gph,blgh->blgp', bb_re, u), jnp.einsum('gph,blgh->blgp', bb_im, u)

        def read(h_re, h_im):
            return jnp.einsum('ghp,blgp->blgh', cr, h_re) - jnp.einsum('ghp,blgp->blgh', ci, h_im)

        hc_re, hc_im = s5_scan(a_re, a_im, *drive(uc), zeros, zeros, reverse)
        end = 0 if reverse else -1
        hl_re, hl_im = s5_scan(a_re, a_im, *drive(ul), hc_re[:, end], hc_im[:, end], reverse)
        ys_l.append(read(hl_re, hl_im))
        if need_ctx:
            ys_c.append(read(hc_re, hc_im))
    out_l = s5_readout(ys_l[0] + ys_l[1], u_l, d, w_glu, b_glu)
    out_c = s5_readout(ys_c[0] + ys_c[1], u_c, d, w_glu, b_glu) if need_ctx else None
    return out_c, out_l


def centred_conv(v, w, b):
    n = v.shape[1]
    pad = CONV_K // 2
    vp = jnp.pad(v, ((0, 0), (pad, CONV_K - 1 - pad), (0, 0)))
    return sum(vp[:, j:j + n] * w[j] for j in range(CONV_K)) + b


def even_mixer(h_c, h_l, w_in, w_out, lam_re, lam_im, log_step, b_re, b_im, c_re, c_im, d, w_glu, b_glu,
               conv_w, conv_b, need_ctx):
    def split(h):
        p = jnp.dot(h, w_in)
        bg, cg, v = jnp.split(p[..., S5_W:], 3, axis=-1)
        return p[..., :S5_W], bg, cg, v
    u_c, bg_c, cg_c, v_c = split(h_c)
    u_l, bg_l, cg_l, v_l = split(h_l)
    s5_c, s5_l = s5_mix(u_c, u_l, lam_re, lam_im, log_step, b_re, b_im, c_re, c_im, d, w_glu, b_glu, need_ctx)
    conv_l = bg_l * centred_conv(cg_l * v_l, conv_w, conv_b)
    y_l = jnp.dot(jnp.concatenate([s5_l, conv_l], axis=-1), w_out)
    if not need_ctx:
        return None, y_l
    conv_c = bg_c * centred_conv(cg_c * v_c, conv_w, conv_b)
    y_c = jnp.dot(jnp.concatenate([s5_c, conv_c], axis=-1), w_out)
    return y_c, y_l


def axial_rope_tables(n):
    rows = n // GRID_W
    row = jnp.repeat(jnp.arange(rows), GRID_W).astype(F32)
    col = jnp.tile(jnp.arange(GRID_W), rows).astype(F32)
    freqs = jnp.power(ROPE_BASE, -jnp.arange(ROPE_PAIRS_AXIS, dtype=F32) / ROPE_PAIRS_AXIS)
    ang = jnp.concatenate([row[:, None] * freqs, col[:, None] * freqs], axis=-1)
    return jnp.cos(ang), jnp.sin(ang)


def apply_rope(x, cos, sin):
    half = ATTN_DH // 2
    cos = cos[None, :, None, None, :].astype(x.dtype)
    sin = sin[None, :, None, None, :].astype(x.dtype)
    x1, x2 = x[..., :half], x[..., half:]
    return jnp.concatenate([x1 * cos - x2 * sin, x1 * sin + x2 * cos], axis=-1)


def diff_attend(q, k, v, lam):
    s = jnp.einsum('bqhcd,bkhcd->bhcqk', q.astype(F32), k) * (ATTN_DH ** -0.5)
    p = jax.nn.softmax(s, axis=-1)
    a = p[:, :, 0] - lam * p[:, :, 1]
    return jnp.einsum('bhqk,bkhe->bqhe', a.astype(v.dtype), v)


def diff_attn_mixer(h_c, h_l, w_qkv, w_o, lam_vecs, subln, lam_init, need_ctx):
    bsz, n_lat = h_l.shape[:2]

    def qkv(h):
        n = h.shape[1]
        q, k, v = jnp.split(jnp.dot(h, w_qkv), 3, axis=-1)
        return (q.reshape(bsz, n, ATTN_H, 2, ATTN_DH), k.reshape(bsz, n, ATTN_H, 2, ATTN_DH),
                v.reshape(bsz, n, ATTN_H, ATTN_VD))
    q_c, k_c, v_c = qkv(h_c)
    q_l, k_l, v_l = qkv(h_l)
    cos, sin = axial_rope_tables(n_lat)
    q_l, k_l = apply_rope(q_l, cos, sin), apply_rope(k_l, cos, sin)
    lv = lam_vecs.astype(F32)
    lam = jnp.exp(jnp.sum(lv[0] * lv[1])) - jnp.exp(jnp.sum(lv[2] * lv[3])) + lam_init

    def finish(o):
        o = rms_norm(o, subln) * (1.0 - lam_init)
        return jnp.dot(o.reshape(o.shape[:2] + (D_MODEL,)), w_o)

    k_all = jnp.concatenate([k_c, k_l], axis=1).astype(F32)
    v_all = jnp.concatenate([v_c, v_l], axis=1)
    n_blk = n_lat // Q_BLOCK
    qb = q_l.reshape(bsz, n_blk, Q_BLOCK, ATTN_H, 2, ATTN_DH).swapaxes(0, 1)
    o_l = lax.map(lambda qq: diff_attend(qq, k_all, v_all, lam), qb)
    o_l = o_l.swapaxes(0, 1).reshape(bsz, n_lat, ATTN_H, ATTN_VD)
    y_l = finish(o_l)
    if not need_ctx:
        return None, y_l
    y_c = finish(diff_attend(q_c, k_c.astype(F32), v_c, lam))
    return y_c, y_l


def setup_inputs(seed: int = 0) -> dict:
    key = jax.random.key(seed)
    ks = jax.random.split(key, 32)
    nrm = jax.random.normal
    lam_im = jnp.broadcast_to(jnp.pi * jnp.arange(S5_P, dtype=F32), (N_EVEN, 2, S5_G, S5_P))
    return {
        'x': nrm(ks[0], (BATCH, SEQ, D_MODEL), F32),
        'c': nrm(ks[1], (BATCH, D_MODEL), F32),
        'ctx': nrm(ks[2], (BATCH, CTX_LEN, D_MODEL), F32),
        'c_ctx': nrm(ks[3], (D_MODEL,), F32),
        'w_mod': nrm(ks[4], (DEPTH, D_MODEL, N_MOD * D_MODEL), F32) * (0.5 * D_MODEL ** -0.5),
        'b_mod': nrm(ks[5], (DEPTH, N_MOD * D_MODEL), F32) * 0.02,
        'norm_g': 1.0 + 0.02 * nrm(ks[6], (DEPTH, 2 * N_SUB, D_MODEL), F32),
        'ffn_wg': nrm(ks[7], (DEPTH, 2, D_MODEL, D_FF), F32) * D_MODEL ** -0.5,
        'ffn_wu': nrm(ks[8], (DEPTH, 2, D_MODEL, D_FF), F32) * D_MODEL ** -0.5,
        'ffn_wd': nrm(ks[9], (DEPTH, 2, D_FF, D_MODEL), F32) * D_FF ** -0.5,
        'mix_w_in': nrm(ks[10], (N_EVEN, D_MODEL, MIX_IN), F32) * D_MODEL ** -0.5,
        'mix_w_out': nrm(ks[11], (N_EVEN, MIX_OUT, D_MODEL), F32) * MIX_OUT ** -0.5,
        's5_lam_re': -0.5 + 0.01 * nrm(ks[12], (N_EVEN, 2, S5_G, S5_P), F32),
        's5_lam_im': lam_im + 0.01 * nrm(ks[13], (N_EVEN, 2, S5_G, S5_P), F32),
        's5_log_step': jax.random.uniform(ks[14], (N_EVEN, 2, S5_G), F32,
                                          minval=math.log(S5_DT_MIN), maxval=math.log(S5_DT_MAX)),
        's5_b_re': nrm(ks[15], (N_EVEN, 2, S5_G, S5_P, S5_H), F32) * (2 * S5_H) ** -0.5,
        's5_b_im': nrm(ks[16], (N_EVEN, 2, S5_G, S5_P, S5_H), F32) * (2 * S5_H) ** -0.5,
        's5_c_re': nrm(ks[17], (N_EVEN, 2, S5_G, S5_H, S5_P), F32) * S5_P ** -0.5,
        's5_c_im': nrm(ks[18], (N_EVEN, 2, S5_G, S5_H, S5_P), F32) * S5_P ** -0.5,
        's5_d': nrm(ks[19], (N_EVEN, S5_W), F32) * 0.3,
        's5_w_glu': nrm(ks[20], (N_EVEN, S5_W, S5_W), F32) * S5_W ** -0.5,
        's5_b_glu': nrm(ks[21], (N_EVEN, S5_W), F32) * 0.02,
        'conv_w': nrm(ks[22], (N_EVEN, CONV_K, CONV_W), F32) * CONV_K ** -0.5,
        'conv_b': nrm(ks[23], (N_EVEN, CONV_W), F32) * 0.02,
        'attn_w_qkv': nrm(ks[24], (N_ODD, D_MODEL, 3 * D_MODEL), F32) * D_MODEL ** -0.5,
        'attn_w_o': nrm(ks[25], (N_ODD, D_MODEL, D_MODEL), F32) * D_MODEL ** -0.5,
        'attn_lambda': nrm(ks[26], (N_ODD, 4, ATTN_DH), F32) * 0.1,
        'attn_subln': 1.0 + 0.02 * nrm(ks[27], (N_ODD, ATTN_VD), F32),
    }


def reference(x, c, ctx, c_ctx, w_mod, b_mod, norm_g, ffn_wg, ffn_wu, ffn_wd, mix_w_in, mix_w_out,
              s5_lam_re, s5_lam_im, s5_log_step, s5_b_re, s5_b_im, s5_c_re, s5_c_im, s5_d, s5_w_glu, s5_b_glu,
              conv_w, conv_b, attn_w_qkv, attn_w_o, attn_lambda, attn_subln):
    for layer in range(DEPTH):
        need_ctx = layer < DEPTH - 1
        m_l = ada_params(c, w_mod[layer], b_mod[layer])[:, None]
        m_c = ada_params(c_ctx, w_mod[layer], b_mod[layer])[None, None]
        g = norm_g[layer]
        x = ffn_sublayer(x, m_l, 0, g[0], g[1], ffn_wg[layer, 0], ffn_wu[layer, 0], ffn_wd[layer, 0])
        ctx = ffn_sublayer(ctx, m_c, 0, g[0], g[1], ffn_wg[layer, 0], ffn_wu[layer, 0], ffn_wd[layer, 0])
        h_l = mod_in(x, g[2], m_l, 1)
        h_c = mod_in(ctx, g[2], m_c, 1)
        i = layer // 2
        if layer % 2 == 0:
            y_c, y_l = even_mixer(h_c, h_l, mix_w_in[i], mix_w_out[i], s5_lam_re[i], s5_lam_im[i], s5_log_step[i],
                                  s5_b_re[i], s5_b_im[i], s5_c_re[i], s5_c_im[i], s5_d[i], s5_w_glu[i], s5_b_glu[i],
                                  conv_w[i], conv_b[i], need_ctx)
        else:
            lam_init = 0.8 - 0.6 * math.exp(-0.3 * layer)
            y_c, y_l = diff_attn_mixer(h_c, h_l, attn_w_qkv[i], attn_w_o[i], attn_lambda[i], attn_subln[i],
                                       lam_init, need_ctx)
        x = mod_out(x, y_l, g[3], m_l, 1, 1.0)
        x = ffn_sublayer(x, m_l, 2, g[4], g[5], ffn_wg[layer, 1], ffn_wu[layer, 1], ffn_wd[layer, 1])
        if need_ctx:
            ctx = mod_out(ctx, y_c, g[3], m_c, 1, 1.0)
            ctx = ffn_sublayer(ctx, m_c, 2, g[4], g[5], ffn_wg[layer, 1], ffn_wu[layer, 1], ffn_wd[layer, 1])
    return x
```

```python
import functools
import math

import jax
import jax.numpy as jnp
from jax import lax
from jax.experimental import pallas as pl
from jax.experimental.pallas import tpu as pltpu

F32 = jnp.float32
BF16 = jnp.bfloat16
EPS = 1e-6
N_MOD = 9
GRID_W = 64
ROPE_BASE = 10000.0
S5_CHUNK = 16
SUBLANES = 8
LANES = 128
VMEM_LIMIT = 56 * 1024 * 1024


def _cparams(sem):
    return pltpu.CompilerParams(dimension_semantics=sem, vmem_limit_bytes=VMEM_LIMIT)


def _dot(a, b):
    return jnp.dot(a, b, preferred_element_type=F32)


def _rms(x, g):
    return x * lax.rsqrt(jnp.mean(x * x, axis=-1, keepdims=True) + EPS) * g


def _pick(n, candidates):
    for c in candidates:
        if n % c == 0:
            return c
    return n


def _ada_kernel(c_ref, w_ref, b_ref, o_ref):
    s = c_ref[...]
    s = s * jax.nn.sigmoid(s)
    o_ref[...] = _dot(s.astype(BF16), w_ref[...].astype(BF16)) + b_ref[...]


def _ada(cc, w_mod, b_mod):
    depth, d, n = w_mod.shape
    tn = _pick(n, (1024, 512, 256, 128))
    return pl.pallas_call(
        _ada_kernel,
        grid=(depth, n // tn),
        in_specs=[pl.BlockSpec((SUBLANES, d), lambda l, j: (0, 0)),
                  pl.BlockSpec((None, d, tn), lambda l, j: (l, 0, j)),
                  pl.BlockSpec((None, 1, tn), lambda l, j: (l, 0, j))],
        out_specs=pl.BlockSpec((None, SUBLANES, tn), lambda l, j: (l, 0, j)),
        out_shape=jax.ShapeDtypeStruct((depth, SUBLANES, n), F32),
        compiler_params=_cparams(("arbitrary", "arbitrary")),
    )(cc, w_mod, b_mod.reshape(depth, 1, n))


def _mod3(mod_ref, sub, d):
    return tuple(mod_ref[:, (3 * sub + k) * d:(3 * sub + k + 1) * d] for k in range(3))


def _ffn_kernel(x_ref, mod_ref, g_ref, wg_ref, wu_ref, wd_ref, o_ref, h_ref, acc_ref, *, sub, d):
    j = pl.program_id(1)
    shift, scale, gate = _mod3(mod_ref, sub, d)

    @pl.when(j == 0)
    def _():
        h = _rms(x_ref[...], g_ref[2 * sub:2 * sub + 1, :]) * (1.0 + scale) + shift
        h_ref[...] = h.astype(BF16)
        acc_ref[...] = jnp.zeros_like(acc_ref)

    h = h_ref[...]
    a = _dot(h, wg_ref[...])
    u = _dot(h, wu_ref[...])
    act = (a * jax.nn.sigmoid(a) * u).astype(BF16)
    acc_ref[...] += _dot(act, wd_ref[...])

    @pl.when(j == pl.num_programs(1) - 1)
    def _():
        y = _rms(acc_ref[...], g_ref[2 * sub + 1:2 * sub + 2, :])
        o_ref[...] = x_ref[...] + 0.5 * gate * y


def _ffn(xt, mod, norm_g, wg, wu, wd, layer, sub, widx, lay):
    r, d = xt.shape
    f = wg.shape[-1]
    tm, tf = lay.tm, _pick(f, (512, 256, 128))
    return pl.pallas_call(
        functools.partial(_ffn_kernel, sub=sub, d=d),
        grid=(r // tm, f // tf),
        in_specs=[pl.BlockSpec((tm, d), lambda i, j: (i, 0)),
                  pl.BlockSpec((None, None, 1, N_MOD * d), lambda i, j: (layer, lay.mod_row(i), 0, 0)),
                  pl.BlockSpec((None, 6, d), lambda i, j: (layer, 0, 0)),
                  pl.BlockSpec((None, None, d, tf), lambda i, j: (layer, widx, 0, j)),
                  pl.BlockSpec((None, None, d, tf), lambda i, j: (layer, widx, 0, j)),
                  pl.BlockSpec((None, None, tf, d), lambda i, j: (layer, widx, j, 0))],
        out_specs=pl.BlockSpec((tm, d), lambda i, j: (i, 0)),
        out_shape=jax.ShapeDtypeStruct((r, d), F32),
        scratch_shapes=[pltpu.VMEM((tm, d), BF16), pltpu.VMEM((tm, d), F32)],
        compiler_params=_cparams(("parallel", "arbitrary")),
    )(xt, mod, norm_g, wg, wu, wd)


def _proj_kernel(x_ref, mod_ref, g_ref, w_ref, cos_ref, sin_ref, o_ref, h_ref, *, d, n_lat_tiles, n_rot_tiles,
                 n_q_tiles, q_scale):
    i, j = pl.program_id(0), pl.program_id(1)
    shift, scale, _ = _mod3(mod_ref, 1, d)

    @pl.when(j == 0)
    def _():
        h = _rms(x_ref[...], g_ref[2:3, :]) * (1.0 + scale) + shift
        h_ref[...] = h.astype(BF16)

    y = _dot(h_ref[...], w_ref[...])
    if n_q_tiles:
        y = y * jnp.where(j < n_q_tiles, q_scale, 1.0)
    rotate = jnp.logical_and(i < n_lat_tiles, j < n_rot_tiles)

    @pl.when(rotate)
    def _():
        tn = y.shape[1]
        lane = lax.broadcasted_iota(jnp.int32, (1, LANES), 1)
        first_half = (lane % 64) < 32
        for c in range(tn // LANES):
            yc = y[:, c * LANES:(c + 1) * LANES]
            partner = jnp.where(first_half, pltpu.roll(yc, LANES - 32, 1), pltpu.roll(yc, 32, 1))
            o_ref[:, c * LANES:(c + 1) * LANES] = (yc * cos_ref[...] + partner * sin_ref[...]).astype(o_ref.dtype)

    @pl.when(jnp.logical_not(rotate))
    def _():
        o_ref[...] = y.astype(o_ref.dtype)


def _proj(xt, mod, norm_g, w, widx, layer, lay, rope=None):
    r, d = xt.shape
    n = w.shape[-1]
    tm = lay.tm
    if rope is None:
        tn = _pick(n, (1024, 512, 256, 128))
        cos = sin = jnp.zeros((tm, LANES), F32)
        n_rot = n_q = 0
        tab_map = lambda i, j: (0, 0)
    else:
        cos, sin, n_rot, n_q = rope
        tn = _pick(math.gcd(n_q, n), (1024, 512, 256, 128))
        tab_map = lambda i, j: (i % lay.tiles_per_seq, 0)
    return pl.pallas_call(
        functools.partial(_proj_kernel, d=d, n_lat_tiles=lay.n_lat_tiles, n_rot_tiles=n_rot // tn,
                          n_q_tiles=n_q // tn, q_scale=lay.q_scale),
        grid=(r // tm, n // tn),
        in_specs=[pl.BlockSpec((tm, d), lambda i, j: (i, 0)),
                  pl.BlockSpec((None, None, 1, N_MOD * d), lambda i, j: (layer, lay.mod_row(i), 0, 0)),
                  pl.BlockSpec((None, 6, d), lambda i, j: (layer, 0, 0)),
                  pl.BlockSpec((None, d, tn), lambda i, j: (widx, 0, j)),
                  pl.BlockSpec((tm, LANES), tab_map),
                  pl.BlockSpec((tm, LANES), tab_map)],
        out_specs=pl.BlockSpec((tm, tn), lambda i, j: (i, j)),
        out_shape=jax.ShapeDtypeStruct((r, n), BF16),
        scratch_shapes=[pltpu.VMEM((tm, d), BF16)],
        compiler_params=_cparams(("parallel", "arbitrary")),
    )(xt, mod, norm_g, w, cos, sin)


def _rope_tables(n):
    rows = n // GRID_W
    row = jnp.repeat(jnp.arange(rows), GRID_W).astype(F32)
    col = jnp.tile(jnp.arange(GRID_W), rows).astype(F32)
    pairs = 16
    freqs = jnp.power(ROPE_BASE, -jnp.arange(pairs, dtype=F32) / pairs)
    ang = jnp.concatenate([row[:, None] * freqs, col[:, None] * freqs], axis=-1)
    cos, sin = jnp.cos(ang), jnp.sin(ang)
    return jnp.tile(cos, (1, 4)), jnp.concatenate([-sin, sin, -sin, sin], axis=-1)


def _s5_matrices(lam_re, lam_im, log_step, b_re, b_im, c_re, c_im, t):
    hp = lax.Precision.HIGHEST
    lam_re, lam_im = lam_re.astype(F32), lam_im.astype(F32)
    dt = jnp.exp(log_step.astype(F32))[..., None]
    lr, li = lam_re * dt, lam_im * dt
    mag = jnp.exp(lr)
    a_re, a_im = mag * jnp.cos(li), mag * jnp.sin(li)
    inv = 1.0 / (lam_re * lam_re + lam_im * lam_im)
    co_re = ((a_re - 1.0) * lam_re + a_im * lam_im) * inv
    co_im = (a_im * lam_re - (a_re - 1.0) * lam_im) * inv
    b_re, b_im = b_re.astype(F32), b_im.astype(F32)
    bb_re = co_re[..., None] * b_re - co_im[..., None] * b_im
    bb_im = co_re[..., None] * b_im + co_im[..., None] * b_re
    c_re, c_im = c_re.astype(F32), c_im.astype(F32)
    tau = jnp.arange(t + 1, dtype=F32)[:, None, None, None]
    pmag = jnp.exp(tau * lr[None])
    pw_re, pw_im = pmag * jnp.cos(tau * li[None]), pmag * jnp.sin(tau * li[None])
    cp_re = c_re[None] * pw_re[:, :, :, None, :] - c_im[None] * pw_im[:, :, :, None, :]
    cp_im = c_re[None] * pw_im[:, :, :, None, :] + c_im[None] * pw_re[:, :, :, None, :]
    kern = (jnp.einsum('tdghp,dgpk->tdghk', cp_re, bb_re, precision=hp)
            - jnp.einsum('tdghp,dgpk->tdghk', cp_im, bb_im, precision=hp))
    g, p, h = b_re.shape[1:]
    s_idx = jnp.arange(t)[:, None]
    t_idx = jnp.arange(t)[None, :]

    def toeplitz(k_dir, lag):
        m = jnp.where((lag >= 0)[:, :, None, None, None], k_dir[jnp.clip(lag, 0, t)], 0.0)
        return m.transpose(2, 0, 4, 1, 3).reshape(g, t * h, t * h)

    mk = toeplitz(kern[:, 0], t_idx - s_idx) + toeplitz(kern[:, 1], s_idx - t_idx)

    def drive(d, exps):
        wr, wi = pw_re[exps, d], pw_im[exps, d]
        re = wr[..., None] * bb_re[d][None] - wi[..., None] * bb_im[d][None]
        im = wr[..., None] * bb_im[d][None] + wi[..., None] * bb_re[d][None]
        f = lambda m: m.transpose(1, 0, 3, 2).reshape(g, t * h, p)
        return f(re), f(im)

    def read(d, exps):
        f = lambda m: m.transpose(1, 3, 0, 2).reshape(g, p, t * h)
        return f(cp_re[exps, d]), f(-cp_im[exps, d])

    ar = jnp.arange(t)
    z_s = jnp.zeros((g // 2, t * h, p), F32)
    z_o = jnp.zeros((g // 2, p, t * h), F32)

    def pair_drive(re, im):
        re, im = re.reshape(g // 2, 2, t * h, p), im.reshape(g // 2, 2, t * h, p)
        top = jnp.concatenate([re[:, 0], z_s, im[:, 0], z_s], axis=-1)
        bot = jnp.concatenate([z_s, re[:, 1], z_s, im[:, 1]], axis=-1)
        return jnp.concatenate([top, bot], axis=1).astype(BF16)

    def pair_read(re, imn):
        re, imn = re.reshape(g // 2, 2, p, t * h), imn.reshape(g // 2, 2, p, t * h)
        rows = [jnp.concatenate([re[:, 0], z_o], axis=-1), jnp.concatenate([z_o, re[:, 1]], axis=-1),
                jnp.concatenate([imn[:, 0], z_o], axis=-1), jnp.concatenate([z_o, imn[:, 1]], axis=-1)]
        return jnp.concatenate(rows, axis=1).astype(BF16)

    ms_f = pair_drive(*drive(0, t - 1 - ar))
    ms_r = pair_drive(*drive(1, ar))
    mo_f = pair_read(*read(0, ar + 1))
    mo_r = pair_read(*read(1, t - ar))
    at = jnp.stack([pw_re[t, 0], pw_im[t, 0], pw_re[t, 1], pw_im[t, 1]], axis=1)
    at = at.reshape(g // 2, 2, 4, p).transpose(0, 2, 1, 3).reshape(g // 2, 4, 2 * p)
    return mk.reshape(g // 2, 2, t * h, t * h).astype(BF16), ms_f, ms_r, mo_f, mo_r, at


def _s5_kernel(uf_ref, mk_ref, msf_ref, msr_ref, mof_ref, mor_ref, a_ref, y_ref, sf_ref, sr_ref, hf_ref, hr_ref, *,
               batch, lat_tiles, ctx_tiles):
    th = mk_ref.shape[-1]
    w = a_ref.shape[-1]
    uf = uf_ref[...]
    sf_ref[...] = _dot(uf, msf_ref[...])
    sr_ref[...] = _dot(uf, msr_ref[...])
    a = a_ref[...]
    af = (a[0:1], a[1:2])
    ab = (a[2:3], a[3:4])
    row_id = lax.broadcasted_iota(jnp.int32, (SUBLANES, 2 * w), 0)

    def tile_step(s_ref, hp_ref, tile, state, coef, rows):
        off = pl.multiple_of(tile * SUBLANES, SUBLANES)
        s = s_ref[pl.ds(off, SUBLANES), :]
        hre, him = state
        are, aim = coef
        prev = jnp.zeros((SUBLANES, 2 * w), F32)
        for i in rows:
            row = jnp.concatenate([hre, him], axis=1)
            prev = jnp.where(row_id == i, jnp.broadcast_to(row, prev.shape), prev)
            sre, sim = s[i:i + 1, :w], s[i:i + 1, w:]
            hre, him = are * hre - aim * him + sre, are * him + aim * hre + sim
        hp_ref[pl.ds(off, SUBLANES), :] = prev
        return hre, him

    zero = jnp.zeros((1, w), F32)
    state = tuple((zero, zero) for _ in range(2 * batch))
    asc, desc = tuple(range(SUBLANES)), tuple(reversed(range(SUBLANES)))

    def phase(first_tile, n_tiles, state):
        def body(k, st):
            out = []
            for b in range(batch):
                out.append(tile_step(sf_ref, hf_ref, first_tile(b) + k, st[2 * b], af, asc))
                out.append(tile_step(sr_ref, hr_ref, first_tile(b) + n_tiles - 1 - k, st[2 * b + 1], ab, desc))
            return tuple(out)
        return lax.fori_loop(0, n_tiles, body, state)

    state = phase(lambda b: batch * lat_tiles + b * ctx_tiles, ctx_tiles, state)
    phase(lambda b: b * lat_tiles, lat_tiles, state)

    carried = _dot(hf_ref[...].astype(BF16), mof_ref[...]) + _dot(hr_ref[...].astype(BF16), mor_ref[...])
    y_ref[:, :th] = _dot(uf[:, :th], mk_ref[0]) + carried[:, :th]
    y_ref[:, th:] = _dot(uf[:, th:], mk_ref[1]) + carried[:, th:]


def _s5(uf, mats, lay):
    mk, ms_f, ms_r, mo_f, mo_r, at = mats
    gp, nc, w2 = uf.shape
    th = mk.shape[-1]
    p4 = ms_f.shape[-1]
    t = S5_CHUNK
    full = lambda *shape: pl.BlockSpec((None,) + shape, lambda q: (q,) + (0,) * len(shape))
    return pl.pallas_call(
        functools.partial(_s5_kernel, batch=lay.batch, lat_tiles=lay.seq // (t * SUBLANES),
                          ctx_tiles=lay.ctx // (t * SUBLANES)),
        grid=(gp,),
        in_specs=[full(nc, w2), full(2, th, th), full(w2, p4), full(w2, p4), full(p4, w2), full(p4, w2),
                  full(4, p4 // 2)],
        out_specs=full(nc, w2),
        out_shape=jax.ShapeDtypeStruct((gp, nc, w2), F32),
        scratch_shapes=[pltpu.VMEM((nc, p4), F32)] * 4,
        compiler_params=_cparams(("parallel",)),
    )(uf, mk, ms_f, ms_r, mo_f, mo_r, at)


def _gelu_tanh(x):
    return 0.5 * x * (1.0 + jnp.tanh(math.sqrt(2.0 / math.pi) * (x + 0.044715 * (x * x * x))))


def _mixout_kernel(x_ref, mod_ref, g_ref, y_ref, u_ref, bg_ref, cg_ref, v_ref, cgp_ref, vp_ref, cgn_ref, vn_ref,
                   kp_ref, kn_ref, d_ref, wglu_ref, bglu_ref, cw_ref, cb_ref, wout_ref, o_ref, *, d, halo):
    _, _, gate = _mod3(mod_ref, 1, d)
    sw = u_ref.shape[1]
    tm = x_ref.shape[0]
    z = _gelu_tanh(y_ref[...] + d_ref[...] * u_ref[...].astype(F32))
    s5 = z * jax.nn.sigmoid(_dot(z.astype(BF16), wglu_ref[...]) + bglu_ref[...])
    gv = cg_ref[...].astype(F32) * v_ref[...].astype(F32)
    gv_before = cgp_ref[halo - 1:halo, :].astype(F32) * vp_ref[halo - 1:halo, :].astype(F32)
    gv_after = cgn_ref[0:1, :].astype(F32) * vn_ref[0:1, :].astype(F32)
    row = lax.broadcasted_iota(jnp.int32, (tm, 1), 0)
    prev = jnp.where(row == 0, gv_before, pltpu.roll(gv, 1, 0)) * kp_ref[...]
    nxt = jnp.where(row == tm - 1, gv_after, pltpu.roll(gv, tm - 1, 0)) * kn_ref[...]
    conv = cw_ref[0:1, :] * prev + cw_ref[1:2, :] * gv + cw_ref[2:3, :] * nxt + cb_ref[...]
    conv = bg_ref[...].astype(F32) * conv
    y = _dot(s5.astype(BF16), wout_ref[:sw, :]) + _dot(conv.astype(BF16), wout_ref[sw:, :])
    o_ref[...] = x_ref[...] + gate * _rms(y, g_ref[3:4, :])


def _mixout(xt, mod, norm_g, ys, p, keep_prev, keep_next, s5_d, w_glu, b_glu, conv_w, conv_b, w_out, layer, i_even,
            lay):
    r, d = xt.shape
    sw = ys.shape[1]
    tm = min(lay.tm, 256)
    halo = 16
    hb = tm // halo
    last_halo = r // halo - 1
    col = lambda c: pl.BlockSpec((tm, sw), lambda i: (i, c))
    before = lambda c: pl.BlockSpec((halo, sw), lambda i: (jnp.maximum(i * hb - 1, 0), c))
    after = lambda c: pl.BlockSpec((halo, sw), lambda i: (jnp.minimum((i + 1) * hb, last_halo), c))
    vec = lambda n: pl.BlockSpec((None, 1, n), lambda i: (i_even, 0, 0))
    lay_m = lay.with_tm(tm)
    return pl.pallas_call(
        functools.partial(_mixout_kernel, d=d, halo=halo),
        grid=(r // tm,),
        in_specs=[pl.BlockSpec((tm, d), lambda i: (i, 0)),
                  pl.BlockSpec((None, None, 1, N_MOD * d), lambda i: (layer, lay_m.mod_row(i), 0, 0)),
                  pl.BlockSpec((None, 6, d), lambda i: (layer, 0, 0)),
                  pl.BlockSpec((tm, sw), lambda i: (i, 0)),
                  col(0), col(1), col(2), col(3), before(2), before(3), after(2), after(3),
                  pl.BlockSpec((tm, 1), lambda i: (i, 0)),
                  pl.BlockSpec((tm, 1), lambda i: (i, 0)),
                  vec(sw),
                  pl.BlockSpec((None, sw, sw), lambda i: (i_even, 0, 0)),
                  vec(sw),
                  pl.BlockSpec((None, 3, sw), lambda i: (i_even, 0, 0)),
                  vec(sw),
                  pl.BlockSpec((None, 2 * sw, d), lambda i: (i_even, 0, 0))],
        out_specs=pl.BlockSpec((tm, d), lambda i: (i, 0)),
        out_shape=jax.ShapeDtypeStruct((r, d), F32),
        compiler_params=_cparams(("parallel",)),
    )(xt, mod, norm_g, ys, p, p, p, p, p, p, p, p, keep_prev, keep_next,
      s5_d.reshape(s5_d.shape[0], 1, sw), w_glu, b_glu.reshape(b_glu.shape[0], 1, sw), conv_w,
      conv_b.reshape(conv_b.shape[0], 1, sw), w_out)


def _attn_kernel(lam_ref, q_ref, kc_ref, vc_ref, k_ref, v_ref, sub_ref, o_ref, q2_ref, m_ref, l_ref, acc_ref, *,
                 n_lat_q, tk, out_scale):
    qi = pl.program_id(2)
    tq, hd = q_ref.shape
    q = q_ref[...]
    lane = lax.broadcasted_iota(jnp.int32, (tq, hd), 1)
    zero = jnp.zeros_like(q)
    q2_ref[:tq, :] = jnp.where(lane < hd // 2, q, zero)
    q2_ref[tq:, :] = jnp.where(lane >= hd // 2, q, zero)
    m_ref[...] = jnp.full_like(m_ref, -jnp.inf)
    l_ref[...] = jnp.zeros_like(l_ref)
    acc_ref[...] = jnp.zeros_like(acc_ref)

    def step(k, v):
        s = lax.dot_general(q2_ref[...], k, (((1,), (1,)), ((), ())), preferred_element_type=F32)
        m_prev = m_ref[...]
        m_new = jnp.maximum(m_prev, jnp.max(s, axis=1, keepdims=True))
        alpha = jnp.exp(m_prev - m_new)
        pr = jnp.exp(s - m_new)
        l_ref[...] = alpha * l_ref[...] + jnp.sum(pr, axis=1, keepdims=True)
        acc_ref[...] = alpha * acc_ref[...] + _dot(pr.astype(BF16), v)
        m_ref[...] = m_new

    step(kc_ref[...], vc_ref[...])

    @pl.when(qi < n_lat_q)
    def _():
        def body(c, carry):
            off = pl.multiple_of(c * tk, tk)
            step(k_ref[pl.ds(off, tk), :], v_ref[pl.ds(off, tk), :])
            return carry
        lax.fori_loop(0, k_ref.shape[0] // tk, body, 0)

    o = acc_ref[...] / l_ref[...]
    o = o[:tq] - lam_ref[0] * o[tq:]
    o_ref[...] = (_rms(o, sub_ref[...]) * out_scale).astype(o_ref.dtype)


def _attn(qkv, lam, subln, lam_init, lay):
    r = qkv.shape[0]
    hd = subln.shape[-1]
    heads = qkv.shape[1] // (3 * hd)
    tq = lay.ctx
    tk = _pick(lay.seq, (512, 256, 128))
    n_lat_q = lay.seq // tq
    lat_blocks = lay.batch * n_lat_q

    def q_map(b, h, qi):
        return (jnp.where(qi < n_lat_q, b * n_lat_q + qi, lat_blocks + b), h)

    return pl.pallas_call(
        functools.partial(_attn_kernel, n_lat_q=n_lat_q, tk=tk, out_scale=1.0 - lam_init),
        grid=(lay.batch, heads, n_lat_q + 1),
        in_specs=[pl.BlockSpec(memory_space=pltpu.SMEM),
                  pl.BlockSpec((tq, hd), q_map),
                  pl.BlockSpec((lay.ctx, hd), lambda b, h, qi: (lat_blocks + b, heads + h)),
                  pl.BlockSpec((lay.ctx, hd), lambda b, h, qi: (lat_blocks + b, 2 * heads + h)),
                  pl.BlockSpec((lay.seq, hd), lambda b, h, qi: (b, heads + h)),
                  pl.BlockSpec((lay.seq, hd), lambda b, h, qi: (b, 2 * heads + h)),
                  pl.BlockSpec((1, hd), lambda b, h, qi: (0, 0))],
        out_specs=pl.BlockSpec((tq, hd), q_map),
        out_shape=jax.ShapeDtypeStruct((r, heads * hd), BF16),
        scratch_shapes=[pltpu.VMEM((2 * tq, hd), BF16), pltpu.VMEM((2 * tq, 1), F32), pltpu.VMEM((2 * tq, 1), F32),
                        pltpu.VMEM((2 * tq, hd), F32)],
        compiler_params=_cparams(("parallel", "parallel", "arbitrary")),
    )(lam, qkv, qkv, qkv, qkv, qkv, subln.reshape(1, hd))


def _attnout_kernel(x_ref, mod_ref, g_ref, a_ref, w_ref, o_ref, *, d):
    _, _, gate = _mod3(mod_ref, 1, d)
    y = _dot(a_ref[...], w_ref[...])
    o_ref[...] = x_ref[...] + gate * _rms(y, g_ref[3:4, :])


def _attnout(xt, mod, norm_g, a, w_o, layer, i_odd, lay):
    r, d = xt.shape
    tm = min(lay.tm, 256)
    lay_m = lay.with_tm(tm)
    return pl.pallas_call(
        functools.partial(_attnout_kernel, d=d),
        grid=(r // tm,),
        in_specs=[pl.BlockSpec((tm, d), lambda i: (i, 0)),
                  pl.BlockSpec((None, None, 1, N_MOD * d), lambda i: (layer, lay_m.mod_row(i), 0, 0)),
                  pl.BlockSpec((None, 6, d), lambda i: (layer, 0, 0)),
                  pl.BlockSpec((tm, a.shape[1]), lambda i: (i, 0)),
                  pl.BlockSpec((None, a.shape[1], d), lambda i: (i_odd, 0, 0))],
        out_specs=pl.BlockSpec((tm, d), lambda i: (i, 0)),
        out_shape=jax.ShapeDtypeStruct((r, d), F32),
        compiler_params=_cparams(("parallel",)),
    )(xt, mod, norm_g, a, w_o)


class _Layout:
    def __init__(self, batch, seq, ctx, tm, q_scale):
        self.batch, self.seq, self.ctx, self.tm, self.q_scale = batch, seq, ctx, tm, q_scale
        self.tiles_per_seq = seq // tm
        self.n_lat_tiles = batch * self.tiles_per_seq

    def with_tm(self, tm):
        return _Layout(self.batch, self.seq, self.ctx, tm, self.q_scale)

    def mod_row(self, i):
        return jnp.minimum(i // self.tiles_per_seq, self.batch)


def kernel(x, c, ctx, c_ctx, w_mod, b_mod, norm_g, ffn_wg, ffn_wu, ffn_wd, mix_w_in, mix_w_out, s5_lam_re, s5_lam_im, s5_log_step, s5_b_re, s5_b_im, s5_c_re, s5_c_im, s5_d, s5_w_glu, s5_b_glu, conv_w, conv_b, attn_w_qkv, attn_w_o, attn_lambda, attn_subln):
    batch, seq, d = x.shape
    n_ctx = ctx.shape[1]
    depth = w_mod.shape[0]
    hd = attn_subln.shape[-1]
    dh = attn_lambda.shape[-1]
    g, p, h = s5_b_re.shape[2:]
    sw = g * h
    t = S5_CHUNK
    assert batch + 1 <= SUBLANES and 2 * p == LANES and conv_w.shape[-1] == sw and mix_w_in.shape[-1] == 4 * sw
    assert seq % (t * SUBLANES) == 0 and n_ctx % (t * SUBLANES) == 0 and seq % n_ctx == 0 and hd == 2 * dh
    tm = _pick(math.gcd(seq, batch * n_ctx), (512, 256, 128, 64, 32, 16))
    lay = _Layout(batch, seq, n_ctx, tm, dh ** -0.5)
    r = batch * (seq + n_ctx)

    xt = jnp.concatenate([x.reshape(batch * seq, d), ctx.reshape(batch * n_ctx, d)], axis=0)
    cc = jnp.zeros((SUBLANES, d), F32).at[:batch].set(c).at[batch].set(c_ctx)
    mod = _ada(cc, w_mod, b_mod).reshape(depth, SUBLANES, 1, N_MOD * d)

    wg, wu, wd = ffn_wg.astype(BF16), ffn_wu.astype(BF16), ffn_wd.astype(BF16)
    w_in, w_out, w_glu = mix_w_in.astype(BF16), mix_w_out.astype(BF16), s5_w_glu.astype(BF16)
    w_qkv, w_o = attn_w_qkv.astype(BF16), attn_w_o.astype(BF16)

    cos, sin = _rope_tables(seq)
    pos = jnp.concatenate([jnp.tile(jnp.arange(seq), batch), jnp.tile(jnp.arange(n_ctx), batch)])
    last = jnp.concatenate([jnp.full((batch * seq,), seq - 1), jnp.full((batch * n_ctx,), n_ctx - 1)])
    keep_prev = (pos != 0).astype(F32)[:, None]
    keep_next = (pos != last).astype(F32)[:, None]
    nc = r // t

    for layer in range(depth):
        i = layer // 2
        xt = _ffn(xt, mod, norm_g, wg, wu, wd, layer, 0, 0, lay)
        if layer % 2 == 0:
            pj = _proj(xt, mod, norm_g, w_in, i, layer, lay)
            mats = _s5_matrices(s5_lam_re[i], s5_lam_im[i], s5_log_step[i], s5_b_re[i], s5_b_im[i],
                                s5_c_re[i], s5_c_im[i], t)
            uf = pj[:, :sw].reshape(nc, t, g // 2, 2, h).transpose(2, 0, 3, 1, 4).reshape(g // 2, nc, 2 * t * h)
            ys = _s5(uf, mats, lay)
            ys = ys.reshape(g // 2, nc, 2, t, h).transpose(1, 3, 0, 2, 4).reshape(r, sw)
            xt = _mixout(xt, mod, norm_g, ys, pj, keep_prev, keep_next, s5_d, w_glu, s5_b_glu, conv_w, conv_b, w_out,
                         layer, i, lay)
        else:
            lam_init = 0.8 - 0.6 * math.exp(-0.3 * layer)
            lv = attn_lambda[i].astype(F32)
            lam = (jnp.exp(jnp.sum(lv[0] * lv[1])) - jnp.exp(jnp.sum(lv[2] * lv[3])) + lam_init).reshape(1)
            qkv = _proj(xt, mod, norm_g, w_qkv, i, layer, lay, rope=(cos, sin, 2 * d, d))
            a = _attn(qkv, lam, attn_subln[i], lam_init, lay)
            xt = _attnout(xt, mod, norm_g, a, w_o, layer, i, lay)
        xt = _ffn(xt, mod, norm_g, wg, wu, wd, layer, 2, 1, lay)
    return xt[:batch * seq].reshape(batch, seq, d)
```

```python
import functools
import math

import jax
import jax.numpy as jnp
from jax import lax
from jax.experimental import pallas as pl
from jax.experimental.pallas import tpu as pltpu

F32 = jnp.float32
BF16 = jnp.bfloat16
EPS = 1e-6
N_MOD = 9
GRID_W = 64
ROPE_BASE = 10000.0
S5_CHUNK = 16
SUBLANES = 8
LANES = 128
VMEM_LIMIT = 56 * 1024 * 1024


def _cparams(sem):
    return pltpu.CompilerParams(dimension_semantics=sem, vmem_limit_bytes=VMEM_LIMIT)


def _dot(a, b):
    return jnp.dot(a, b, preferred_element_type=F32)


def _rms(x, g):
    return x * lax.rsqrt(jnp.mean(x * x, axis=-1, keepdims=True) + EPS) * g


def _pick(n, candidates):
    for c in candidates:
        if n % c == 0:
            return c
    return n


def _ada_kernel(c_ref, w_ref, b_ref, o_ref):
    s = c_ref[...]
    s = s * jax.nn.sigmoid(s)
    o_ref[...] = _dot(s.astype(BF16), w_ref[...].astype(BF16)) + b_ref[...]


def _ada(cc, w_mod, b_mod):
    depth, d, n = w_mod.shape
    tn = _pick(n, (1024, 512, 256, 128))
    return pl.pallas_call(
        _ada_kernel,
        grid=(depth, n // tn),
        in_specs=[pl.BlockSpec((SUBLANES, d), lambda l, j: (0, 0)),
                  pl.BlockSpec((None, d, tn), lambda l, j: (l, 0, j)),
                  pl.BlockSpec((None, 1, tn), lambda l, j: (l, 0, j))],
        out_specs=pl.BlockSpec((None, SUBLANES, tn), lambda l, j: (l, 0, j)),
        out_shape=jax.ShapeDtypeStruct((depth, SUBLANES, n), F32),
        compiler_params=_cparams(("arbitrary", "arbitrary")),
    )(cc, w_mod, b_mod.reshape(depth, 1, n))


def _mod3(mod_ref, sub, d):
    return tuple(mod_ref[:, (3 * sub + k) * d:(3 * sub + k + 1) * d] for k in range(3))


def _ffn_kernel(x_ref, mod_ref, g_ref, wg_ref, wu_ref, wd_ref, o_ref, h_ref, acc_ref, *, sub, d):
    j = pl.program_id(1)
    shift, scale, gate = _mod3(mod_ref, sub, d)

    @pl.when(j == 0)
    def _():
        h = _rms(x_ref[...], g_ref[2 * sub:2 * sub + 1, :]) * (1.0 + scale) + shift
        h_ref[...] = h.astype(BF16)
        acc_ref[...] = jnp.zeros_like(acc_ref)

    h = h_ref[...]
    a = _dot(h, wg_ref[...])
    u = _dot(h, wu_ref[...])
    act = (a * jax.nn.sigmoid(a) * u).astype(BF16)
    acc_ref[...] += _dot(act, wd_ref[...])

    @pl.when(j == pl.num_programs(1) - 1)
    def _():
        y = _rms(acc_ref[...], g_ref[2 * sub + 1:2 * sub + 2, :])
        o_ref[...] = x_ref[...] + 0.5 * gate * y


def _ffn(xt, mod, norm_g, wg, wu, wd, layer, sub, widx, lay):
    r, d = xt.shape
    f = wg.shape[-1]
    tm, tf = lay.tm, _pick(f, (512, 256, 128))
    return pl.pallas_call(
        functools.partial(_ffn_kernel, sub=sub, d=d),
        grid=(r // tm, f // tf),
        in_specs=[pl.BlockSpec((tm, d), lambda i, j: (i, 0)),
                  pl.BlockSpec((None, None, 1, N_MOD * d), lambda i, j: (layer, lay.mod_row(i), 0, 0)),
                  pl.BlockSpec((None, 6, d), lambda i, j: (layer, 0, 0)),
                  pl.BlockSpec((None, None, d, tf), lambda i, j: (layer, widx, 0, j)),
                  pl.BlockSpec((None, None, d, tf), lambda i, j: (layer, widx, 0, j)),
                  pl.BlockSpec((None, None, tf, d), lambda i, j: (layer, widx, j, 0))],
        out_specs=pl.BlockSpec((tm, d), lambda i, j: (i, 0)),
        out_shape=jax.ShapeDtypeStruct((r, d), F32),
        scratch_shapes=[pltpu.VMEM((tm, d), BF16), pltpu.VMEM((tm, d), F32)],
        compiler_params=_cparams(("parallel", "arbitrary")),
    )(xt, mod, norm_g, wg, wu, wd)


def _proj_kernel(x_ref, mod_ref, g_ref, w_ref, cos_ref, sin_ref, o_ref, h_ref, *, d, n_lat_tiles, n_rot_tiles,
                 n_q_tiles, q_scale):
    i, j = pl.program_id(0), pl.program_id(1)
    shift, scale, _ = _mod3(mod_ref, 1, d)

    @pl.when(j == 0)
    def _():
        h = _rms(x_ref[...], g_ref[2:3, :]) * (1.0 + scale) + shift
        h_ref[...] = h.astype(BF16)

    y = _dot(h_ref[...], w_ref[...])
    if n_q_tiles:
        y = y * jnp.where(j < n_q_tiles, q_scale, 1.0)
    rotate = jnp.logical_and(i < n_lat_tiles, j < n_rot_tiles)

    @pl.when(rotate)
    def _():
        tn = y.shape[1]
        lane = lax.broadcasted_iota(jnp.int32, (1, LANES), 1)
        first_half = (lane % 64) < 32
        for c in range(tn // LANES):
            yc = y[:, c * LANES:(c + 1) * LANES]
            partner = jnp.where(first_half, pltpu.roll(yc, LANES - 32, 1), pltpu.roll(yc, 32, 1))
            o_ref[:, c * LANES:(c + 1) * LANES] = (yc * cos_ref[...] + partner * sin_ref[...]).astype(o_ref.dtype)

    @pl.when(jnp.logical_not(rotate))
    def _():
        o_ref[...] = y.astype(o_ref.dtype)


def _proj(xt, mod, norm_g, w, widx, layer, lay, rope=None):
    r, d = xt.shape
    n = w.shape[-1]
    tm = lay.tm
    if rope is None:
        tn = _pick(n, (1024, 512, 256, 128))
        cos = sin = jnp.zeros((tm, LANES), F32)
        n_rot = n_q = 0
        tab_map = lambda i, j: (0, 0)
    else:
        cos, sin, n_rot, n_q = rope
        tn = _pick(math.gcd(n_q, n), (1024, 512, 256, 128))
        tab_map = lambda i, j: (i % lay.tiles_per_seq, 0)
    return pl.pallas_call(
        functools.partial(_proj_kernel, d=d, n_lat_tiles=lay.n_lat_tiles, n_rot_tiles=n_rot // tn,
                          n_q_tiles=n_q // tn, q_scale=lay.q_scale),
        grid=(r // tm, n // tn),
        in_specs=[pl.BlockSpec((tm, d), lambda i, j: (i, 0)),
                  pl.BlockSpec((None, None, 1, N_MOD * d), lambda i, j: (layer, lay.mod_row(i), 0, 0)),
                  pl.BlockSpec((None, 6, d), lambda i, j: (layer, 0, 0)),
                  pl.BlockSpec((None, d, tn), lambda i, j: (widx, 0, j)),
                  pl.BlockSpec((tm, LANES), tab_map),
                  pl.BlockSpec((tm, LANES), tab_map)],
        out_specs=pl.BlockSpec((tm, tn), lambda i, j: (i, j)),
        out_shape=jax.ShapeDtypeStruct((r, n), BF16),
        scratch_shapes=[pltpu.VMEM((tm, d), BF16)],
        compiler_params=_cparams(("parallel", "arbitrary")),
    )(xt, mod, norm_g, w, cos, sin)


def _rope_tables(n):
    rows = n // GRID_W
    row = jnp.repeat(jnp.arange(rows), GRID_W).astype(F32)
    col = jnp.tile(jnp.arange(GRID_W), rows).astype(F32)
    pairs = 16
    freqs = jnp.power(ROPE_BASE, -jnp.arange(pairs, dtype=F32) / pairs)
    ang = jnp.concatenate([row[:, None] * freqs, col[:, None] * freqs], axis=-1)
    cos, sin = jnp.cos(ang), jnp.sin(ang)
    return jnp.tile(cos, (1, 4)), jnp.concatenate([-sin, sin, -sin, sin], axis=-1)


def _s5_matrices(lam_re, lam_im, log_step, b_re, b_im, c_re, c_im, t):
    hp = lax.Precision.HIGHEST
    lam_re, lam_im = lam_re.astype(F32), lam_im.astype(F32)
    dt = jnp.exp(log_step.astype(F32))[..., None]
    lr, li = lam_re * dt, lam_im * dt
    mag = jnp.exp(lr)
    a_re, a_im = mag * jnp.cos(li), mag * jnp.sin(li)
    inv = 1.0 / (lam_re * lam_re + lam_im * lam_im)
    co_re = ((a_re - 1.0) * lam_re + a_im * lam_im) * inv
    co_im = (a_im * lam_re - (a_re - 1.0) * lam_im) * inv
    b_re, b_im = b_re.astype(F32), b_im.astype(F32)
    bb_re = co_re[..., None] * b_re - co_im[..., None] * b_im
    bb_im = co_re[..., None] * b_im + co_im[..., None] * b_re
    c_re, c_im = c_re.astype(F32), c_im.astype(F32)
    tau = jnp.arange(t + 1, dtype=F32)[:, None, None, None]
    pmag = jnp.exp(tau * lr[None])
    pw_re, pw_im = pmag * jnp.cos(tau * li[None]), pmag * jnp.sin(tau * li[None])
    cp_re = c_re[None] * pw_re[:, :, :, None, :] - c_im[None] * pw_im[:, :, :, None, :]
    cp_im = c_re[None] * pw_im[:, :, :, None, :] + c_im[None] * pw_re[:, :, :, None, :]
    kern = (jnp.einsum('tdghp,dgpk->tdghk', cp_re, bb_re, precision=hp)
            - jnp.einsum('tdghp,dgpk->tdghk', cp_im, bb_im, precision=hp))
    g, p, h = b_re.shape[1:]
    s_idx = jnp.arange(t)[:, None]
    t_idx = jnp.arange(t)[None, :]

    def toeplitz(k_dir, lag):
        m = jnp.where((lag >= 0)[:, :, None, None, None], k_dir[jnp.clip(lag, 0, t)], 0.0)
        return m.transpose(2, 0, 4, 1, 3).reshape(g, t * h, t * h)

    mk = toeplitz(kern[:, 0], t_idx - s_idx) + toeplitz(kern[:, 1], s_idx - t_idx)

    def drive(d, exps):
        wr, wi = pw_re[exps, d], pw_im[exps, d]
        re = wr[..., None] * bb_re[d][None] - wi[..., None] * bb_im[d][None]
        im = wr[..., None] * bb_im[d][None] + wi[..., None] * bb_re[d][None]
        f = lambda m: m.transpose(1, 0, 3, 2).reshape(g, t * h, p)
        return f(re), f(im)

    def read(d, exps):
        f = lambda m: m.transpose(1, 3, 0, 2).reshape(g, p, t * h)
        return f(cp_re[exps, d]), f(-cp_im[exps, d])

    ar = jnp.arange(t)
    z_s = jnp.zeros((g // 2, t * h, p), F32)
    z_o = jnp.zeros((g // 2, p, t * h), F32)

    def pair_drive(re, im):
        re, im = re.reshape(g // 2, 2, t * h, p), im.reshape(g // 2, 2, t * h, p)
        top = jnp.concatenate([re[:, 0], z_s, im[:, 0], z_s], axis=-1)
        bot = jnp.concatenate([z_s, re[:, 1], z_s, im[:, 1]], axis=-1)
        return jnp.concatenate([top, bot], axis=1).astype(BF16)

    def pair_read(re, imn):
        re, imn = re.reshape(g // 2, 2, p, t * h), imn.reshape(g // 2, 2, p, t * h)
        rows = [jnp.concatenate([re[:, 0], z_o], axis=-1), jnp.concatenate([z_o, re[:, 1]], axis=-1),
                jnp.concatenate([imn[:, 0], z_o], axis=-1), jnp.concatenate([z_o, imn[:, 1]], axis=-1)]
        return jnp.concatenate(rows, axis=1).astype(BF16)

    ms_f = pair_drive(*drive(0, t - 1 - ar))
    ms_r = pair_drive(*drive(1, ar))
    mo_f = pair_read(*read(0, ar + 1))
    mo_r = pair_read(*read(1, t - ar))
    at = jnp.stack([pw_re[t, 0], pw_im[t, 0], pw_re[t, 1], pw_im[t, 1]], axis=1)
    at = at.reshape(g // 2, 2, 4, p).transpose(0, 2, 1, 3).reshape(g // 2, 4, 2 * p)
    return mk.reshape(g // 2, 2, t * h, t * h).astype(BF16), ms_f, ms_r, mo_f, mo_r, at


def _s5_kernel(uf_ref, mk_ref, msf_ref, msr_ref, mof_ref, mor_ref, a_ref, y_ref, sf_ref, sr_ref, hf_ref, hr_ref, *,
               batch, lat_tiles, ctx_tiles):
    th = mk_ref.shape[-1]
    w = a_ref.shape[-1]
    uf = uf_ref[...]
    sf_ref[...] = _dot(uf, msf_ref[...])
    sr_ref[...] = _dot(uf, msr_ref[...])
    a = a_ref[...]
    af = (a[0:1], a[1:2])
    ab = (a[2:3], a[3:4])
    row_id = lax.broadcasted_iota(jnp.int32, (SUBLANES, 2 * w), 0)

    def tile_step(s_ref, hp_ref, tile, state, coef, rows):
        off = pl.multiple_of(tile * SUBLANES, SUBLANES)
        s = s_ref[pl.ds(off, SUBLANES), :]
        hre, him = state
        are, aim = coef
        prev = jnp.zeros((SUBLANES, 2 * w), F32)
        for i in rows:
            row = jnp.concatenate([hre, him], axis=1)
            prev = jnp.where(row_id == i, jnp.broadcast_to(row, prev.shape), prev)
            sre, sim = s[i:i + 1, :w], s[i:i + 1, w:]
            hre, him = are * hre - aim * him + sre, are * him + aim * hre + sim
        hp_ref[pl.ds(off, SUBLANES), :] = prev
        return hre, him

    zero = jnp.zeros((1, w), F32)
    state = tuple((zero, zero) for _ in range(2 * batch))
    asc, desc = tuple(range(SUBLANES)), tuple(reversed(range(SUBLANES)))

    def phase(first_tile, n_tiles, state):
        def body(k, st):
            out = []
            for b in range(batch):
                out.append(tile_step(sf_ref, hf_ref, first_tile(b) + k, st[2 * b], af, asc))
                out.append(tile_step(sr_ref, hr_ref, first_tile(b) + n_tiles - 1 - k, st[2 * b + 1], ab, desc))
            return tuple(out)
        return lax.fori_loop(0, n_tiles, body, state)

    state = phase(lambda b: batch * lat_tiles + b * ctx_tiles, ctx_tiles, state)
    phase(lambda b: b * lat_tiles, lat_tiles, state)

    carried = _dot(hf_ref[...].astype(BF16), mof_ref[...]) + _dot(hr_ref[...].astype(BF16), mor_ref[...])
    y_ref[:, :th] = _dot(uf[:, :th], mk_ref[0]) + carried[:, :th]
    y_ref[:, th:] = _dot(uf[:, th:], mk_ref[1]) + carried[:, th:]


def _s5(uf, mats, lay):
    mk, ms_f, ms_r, mo_f, mo_r, at = mats
    gp, nc, w2 = uf.shape
    th = mk.shape[-1]
    p4 = ms_f.shape[-1]
    t = S5_CHUNK
    full = lambda *shape: pl.BlockSpec((None,) + shape, lambda q: (q,) + (0,) * len(shape))
    return pl.pallas_call(
        functools.partial(_s5_kernel, batch=lay.batch, lat_tiles=lay.seq // (t * SUBLANES),
                          ctx_tiles=lay.ctx // (t * SUBLANES)),
        grid=(gp,),
        in_specs=[full(nc, w2), full(2, th, th), full(w2, p4), full(w2, p4), full(p4, w2), full(p4, w2),
                  full(4, p4 // 2)],
        out_specs=full(nc, w2),
        out_shape=jax.ShapeDtypeStruct((gp, nc, w2), F32),
        scratch_shapes=[pltpu.VMEM((nc, p4), F32)] * 4,
        compiler_params=_cparams(("parallel",)),
    )(uf, mk, ms_f, ms_r, mo_f, mo_r, at)


def _gelu_tanh(x):
    return 0.5 * x * (1.0 + jnp.tanh(math.sqrt(2.0 / math.pi) * (x + 0.044715 * (x * x * x))))


def _mixout_kernel(x_ref, mod_ref, g_ref, y_ref, u_ref, bg_ref, cg_ref, v_ref, cgp_ref, vp_ref, cgn_ref, vn_ref,
                   kp_ref, kn_ref, d_ref, wglu_ref, bglu_ref, cw_ref, cb_ref, wout_ref, o_ref, *, d, halo):
    _, _, gate = _mod3(mod_ref, 1, d)
    sw = u_ref.shape[1]
    tm = x_ref.shape[0]
    z = _gelu_tanh(y_ref[...] + d_ref[...] * u_ref[...].astype(F32))
    s5 = z * jax.nn.sigmoid(_dot(z.astype(BF16), wglu_ref[...]) + bglu_ref[...])
    gv = cg_ref[...].astype(F32) * v_ref[...].astype(F32)
    gv_before = cgp_ref[halo - 1:halo, :].astype(F32) * vp_ref[halo - 1:halo, :].astype(F32)
    gv_after = cgn_ref[0:1, :].astype(F32) * vn_ref[0:1, :].astype(F32)
    row = lax.broadcasted_iota(jnp.int32, (tm, 1), 0)
    prev = jnp.where(row == 0, gv_before, pltpu.roll(gv, 1, 0)) * kp_ref[...]
    nxt = jnp.where(row == tm - 1, gv_after, pltpu.roll(gv, tm - 1, 0)) * kn_ref[...]
    conv = cw_ref[0:1, :] * prev + cw_ref[1:2, :] * gv + cw_ref[2:3, :] * nxt + cb_ref[...]
    conv = bg_ref[...].astype(F32) * conv
    y = _dot(s5.astype(BF16), wout_ref[:sw, :]) + _dot(conv.astype(BF16), wout_ref[sw:, :])
    o_ref[...] = x_ref[...] + gate * _rms(y, g_ref[3:4, :])


def _mixout(xt, mod, norm_g, ys, p, keep_prev, keep_next, s5_d, w_glu, b_glu, conv_w, conv_b, w_out, layer, i_even,
            lay):
    r, d = xt.shape
    sw = ys.shape[1]
    tm = min(lay.tm, 256)
    halo = 16
    hb = tm // halo
    last_halo = r // halo - 1
    col = lambda c: pl.BlockSpec((tm, sw), lambda i: (i, c))
    before = lambda c: pl.BlockSpec((halo, sw), lambda i: (jnp.maximum(i * hb - 1, 0), c))
    after = lambda c: pl.BlockSpec((halo, sw), lambda i: (jnp.minimum((i + 1) * hb, last_halo), c))
    vec = lambda n: pl.BlockSpec((None, 1, n), lambda i: (i_even, 0, 0))
    lay_m = lay.with_tm(tm)
    return pl.pallas_call(
        functools.partial(_mixout_kernel, d=d, halo=halo),
        grid=(r // tm,),
        in_specs=[pl.BlockSpec((tm, d), lambda i: (i, 0)),
                  pl.BlockSpec((None, None, 1, N_MOD * d), lambda i: (layer, lay_m.mod_row(i), 0, 0)),
                  pl.BlockSpec((None, 6, d), lambda i: (layer, 0, 0)),
                  pl.BlockSpec((tm, sw), lambda i: (i, 0)),
                  col(0), col(1), col(2), col(3), before(2), before(3), after(2), after(3),
                  pl.BlockSpec((tm, 1), lambda i: (i, 0)),
                  pl.BlockSpec((tm, 1), lambda i: (i, 0)),
                  vec(sw),
                  pl.BlockSpec((None, sw, sw), lambda i: (i_even, 0, 0)),
                  vec(sw),
                  pl.BlockSpec((None, 3, sw), lambda i: (i_even, 0, 0)),
                  vec(sw),
                  pl.BlockSpec((None, 2 * sw, d), lambda i: (i_even, 0, 0))],
        out_specs=pl.BlockSpec((tm, d), lambda i: (i, 0)),
        out_shape=jax.ShapeDtypeStruct((r, d), F32),
        compiler_params=_cparams(("parallel",)),
    )(xt, mod, norm_g, ys, p, p, p, p, p, p, p, p, keep_prev, keep_next,
      s5_d.reshape(s5_d.shape[0], 1, sw), w_glu, b_glu.reshape(b_glu.shape[0], 1, sw), conv_w,
      conv_b.reshape(conv_b.shape[0], 1, sw), w_out)


def _attn_kernel(lam_ref, qt_ref, kc_ref, vct_ref, k_ref, vt_ref, sub_ref, o_ref, q2_ref, s_ref, m_ref, al_ref, l_ref,
                 acc_ref, *, n_lat_q, tk, out_scale):
    qi = pl.program_id(2)
    hd, tq = qt_ref.shape
    qt = qt_ref[...]
    row = lax.broadcasted_iota(jnp.int32, (hd, tq), 0)
    zero = jnp.zeros_like(qt)
    q2_ref[:, :tq] = jnp.where(row < hd // 2, qt, zero)
    q2_ref[:, tq:] = jnp.where(row >= hd // 2, qt, zero)

    s = _dot(kc_ref[...], q2_ref[...])
    m0 = jnp.max(s, axis=0, keepdims=True)
    pr = jnp.exp2(s - m0)
    m_ref[...] = m0
    l_ref[...] = jnp.sum(pr, axis=0, keepdims=True)
    acc_ref[...] = _dot(vct_ref[...], pr.astype(BF16))

    @pl.when(qi < n_lat_q)
    def _():
        n_keys = k_ref.shape[0]

        def scores(off):
            return _dot(k_ref[pl.ds(pl.multiple_of(off, tk), tk), :], q2_ref[...])

        def fold_max(buf):
            m_prev = m_ref[...]
            m_new = jnp.maximum(m_prev, jnp.max(s_ref[buf], axis=0, keepdims=True))
            al_ref[...] = jnp.exp2(m_prev - m_new)
            m_ref[...] = m_new

        def stage(cur, nxt, off_cur, off_nxt):
            s_ref[nxt] = scores(off_nxt)
            alpha = al_ref[...]
            pr = jnp.exp2(s_ref[cur] - m_ref[...])
            l_ref[...] = alpha * l_ref[...] + jnp.sum(pr, axis=0, keepdims=True)
            pv = _dot(vt_ref[:, pl.ds(pl.multiple_of(off_cur, tk), tk)], pr.astype(BF16))
            fold_max(nxt)
            acc_ref[...] = alpha * acc_ref[...] + pv

        s_ref[0] = scores(0)
        fold_max(0)

        def body(c, carry):
            off = 2 * c * tk
            stage(0, 1, off, off + tk)
            stage(1, 0, off + tk, jnp.minimum(off + 2 * tk, n_keys - tk))
            return carry
        lax.fori_loop(0, n_keys // (2 * tk), body, 0)

    o = acc_ref[...] * (1.0 / l_ref[...])
    o = o[:, :tq] - lam_ref[0] * o[:, tq:]
    o = o * lax.rsqrt(jnp.mean(o * o, axis=0, keepdims=True) + EPS)
    o = o * jnp.tile(sub_ref[...], (1, tq // LANES)) * out_scale
    o_ref[...] = o.T.astype(o_ref.dtype)


def _attn(qkv, lam, subln, lam_init, lay):
    r = qkv.shape[0]
    hd = subln.shape[-1]
    heads = qkv.shape[1] // (3 * hd)
    d = heads * hd
    tq = lay.ctx
    tk = _pick(lay.seq, (1024, 512, 256)) // 2
    n_lat_q = lay.seq // tq
    lat_blocks = lay.batch * n_lat_q
    qt = qkv[:, :d].T
    vt = qkv[:, 2 * d:].T

    def q_block(b, qi):
        return jnp.where(qi < n_lat_q, b * n_lat_q + qi, lat_blocks + b)

    return pl.pallas_call(
        functools.partial(_attn_kernel, n_lat_q=n_lat_q, tk=tk, out_scale=1.0 - lam_init),
        grid=(lay.batch, heads, n_lat_q + 1),
        in_specs=[pl.BlockSpec(memory_space=pltpu.SMEM),
                  pl.BlockSpec((hd, tq), lambda b, h, qi: (h, q_block(b, qi))),
                  pl.BlockSpec((lay.ctx, hd), lambda b, h, qi: (lat_blocks + b, heads + h)),
                  pl.BlockSpec((hd, lay.ctx), lambda b, h, qi: (h, lat_blocks + b)),
                  pl.BlockSpec((lay.seq, hd), lambda b, h, qi: (b, heads + h)),
                  pl.BlockSpec((hd, lay.seq), lambda b, h, qi: (h, b)),
                  pl.BlockSpec((hd, LANES), lambda b, h, qi: (0, 0))],
        out_specs=pl.BlockSpec((tq, hd), lambda b, h, qi: (q_block(b, qi), h)),
        out_shape=jax.ShapeDtypeStruct((r, d), BF16),
        scratch_shapes=[pltpu.VMEM((hd, 2 * tq), BF16), pltpu.VMEM((2, tk, 2 * tq), F32)]
        + [pltpu.VMEM((1, 2 * tq), F32)] * 3 + [pltpu.VMEM((hd, 2 * tq), F32)],
        compiler_params=_cparams(("parallel", "parallel", "arbitrary")),
    )(lam, qt, qkv, vt, qkv, vt, jnp.broadcast_to(subln.reshape(hd, 1), (hd, LANES)))


def _attnout_kernel(x_ref, mod_ref, g_ref, a_ref, w_ref, o_ref, *, d):
    _, _, gate = _mod3(mod_ref, 1, d)
    y = _dot(a_ref[...], w_ref[...])
    o_ref[...] = x_ref[...] + gate * _rms(y, g_ref[3:4, :])


def _attnout(xt, mod, norm_g, a, w_o, layer, i_odd, lay):
    r, d = xt.shape
    tm = min(lay.tm, 256)
    lay_m = lay.with_tm(tm)
    return pl.pallas_call(
        functools.partial(_attnout_kernel, d=d),
        grid=(r // tm,),
        in_specs=[pl.BlockSpec((tm, d), lambda i: (i, 0)),
                  pl.BlockSpec((None, None, 1, N_MOD * d), lambda i: (layer, lay_m.mod_row(i), 0, 0)),
                  pl.BlockSpec((None, 6, d), lambda i: (layer, 0, 0)),
                  pl.BlockSpec((tm, a.shape[1]), lambda i: (i, 0)),
                  pl.BlockSpec((None, a.shape[1], d), lambda i: (i_odd, 0, 0))],
        out_specs=pl.BlockSpec((tm, d), lambda i: (i, 0)),
        out_shape=jax.ShapeDtypeStruct((r, d), F32),
        compiler_params=_cparams(("parallel",)),
    )(xt, mod, norm_g, a, w_o)


class _Layout:
    def __init__(self, batch, seq, ctx, tm, q_scale):
        self.batch, self.seq, self.ctx, self.tm, self.q_scale = batch, seq, ctx, tm, q_scale
        self.tiles_per_seq = seq // tm
        self.n_lat_tiles = batch * self.tiles_per_seq

    def with_tm(self, tm):
        return _Layout(self.batch, self.seq, self.ctx, tm, self.q_scale)

    def mod_row(self, i):
        return jnp.minimum(i // self.tiles_per_seq, self.batch)


def kernel(x, c, ctx, c_ctx, w_mod, b_mod, norm_g, ffn_wg, ffn_wu, ffn_wd, mix_w_in, mix_w_out, s5_lam_re, s5_lam_im, s5_log_step, s5_b_re, s5_b_im, s5_c_re, s5_c_im, s5_d, s5_w_glu, s5_b_glu, conv_w, conv_b, attn_w_qkv, attn_w_o, attn_lambda, attn_subln):
    batch, seq, d = x.shape
    n_ctx = ctx.shape[1]
    depth = w_mod.shape[0]
    hd = attn_subln.shape[-1]
    dh = attn_lambda.shape[-1]
    g, p, h = s5_b_re.shape[2:]
    sw = g * h
    t = S5_CHUNK
    assert batch + 1 <= SUBLANES and 2 * p == LANES and conv_w.shape[-1] == sw and mix_w_in.shape[-1] == 4 * sw
    assert seq % (t * SUBLANES) == 0 and n_ctx % (t * SUBLANES) == 0 and seq % n_ctx == 0 and hd == 2 * dh
    tm = _pick(math.gcd(seq, batch * n_ctx), (512, 256, 128, 64, 32, 16))
    lay = _Layout(batch, seq, n_ctx, tm, dh ** -0.5 * math.log2(math.e))
    r = batch * (seq + n_ctx)

    xt = jnp.concatenate([x.reshape(batch * seq, d), ctx.reshape(batch * n_ctx, d)], axis=0)
    cc = jnp.zeros((SUBLANES, d), F32).at[:batch].set(c).at[batch].set(c_ctx)
    mod = _ada(cc, w_mod, b_mod).reshape(depth, SUBLANES, 1, N_MOD * d)

    wg, wu, wd = ffn_wg.astype(BF16), ffn_wu.astype(BF16), ffn_wd.astype(BF16)
    w_in, w_out, w_glu = mix_w_in.astype(BF16), mix_w_out.astype(BF16), s5_w_glu.astype(BF16)
    w_qkv, w_o = attn_w_qkv.astype(BF16), attn_w_o.astype(BF16)

    cos, sin = _rope_tables(seq)
    pos = jnp.concatenate([jnp.tile(jnp.arange(seq), batch), jnp.tile(jnp.arange(n_ctx), batch)])
    last = jnp.concatenate([jnp.full((batch * seq,), seq - 1), jnp.full((batch * n_ctx,), n_ctx - 1)])
    keep_prev = (pos != 0).astype(F32)[:, None]
    keep_next = (pos != last).astype(F32)[:, None]
    nc = r // t

    for layer in range(depth):
        i = layer // 2
        xt = _ffn(xt, mod, norm_g, wg, wu, wd, layer, 0, 0, lay)
        if layer % 2 == 0:
            pj = _proj(xt, mod, norm_g, w_in, i, layer, lay)
            mats = _s5_matrices(s5_lam_re[i], s5_lam_im[i], s5_log_step[i], s5_b_re[i], s5_b_im[i],
                                s5_c_re[i], s5_c_im[i], t)
            uf = pj[:, :sw].reshape(nc, t, g // 2, 2, h).transpose(2, 0, 3, 1, 4).reshape(g // 2, nc, 2 * t * h)
            ys = _s5(uf, mats, lay)
            ys = ys.reshape(g // 2, nc, 2, t, h).transpose(1, 3, 0, 2, 4).reshape(r, sw)
            xt = _mixout(xt, mod, norm_g, ys, pj, keep_prev, keep_next, s5_d, w_glu, s5_b_glu, conv_w, conv_b, w_out,
                         layer, i, lay)
        else:
            lam_init = 0.8 - 0.6 * math.exp(-0.3 * layer)
            lv = attn_lambda[i].astype(F32)
            lam = (jnp.exp(jnp.sum(lv[0] * lv[1])) - jnp.exp(jnp.sum(lv[2] * lv[3])) + lam_init).reshape(1)
            qkv = _proj(xt, mod, norm_g, w_qkv, i, layer, lay, rope=(cos, sin, 2 * d, d))
            a = _attn(qkv, lam, attn_subln[i], lam_init, lay)
            xt = _attnout(xt, mod, norm_g, a, w_o, layer, i, lay)
        xt = _ffn(xt, mod, norm_g, wg, wu, wd, layer, 2, 1, lay)
    return xt[:batch * seq].reshape(batch, seq, d)
```

```python
import functools
import math

import jax
import jax.numpy as jnp
from jax import lax
from jax.experimental import pallas as pl
from jax.experimental.pallas import tpu as pltpu

F32 = jnp.float32
BF16 = jnp.bfloat16
EPS = 1e-6
N_MOD = 9
GRID_W = 64
ROPE_BASE = 10000.0
S5_CHUNK = 16
S5_BLOCK = 4
ATTN_UNROLL = 4
SUBLANES = 8
LANES = 128
VMEM_LIMIT = 56 * 1024 * 1024


def _cparams(sem):
    return pltpu.CompilerParams(dimension_semantics=sem, vmem_limit_bytes=VMEM_LIMIT)


def _dot(a, b):
    return jnp.dot(a, b, preferred_element_type=F32)


def _rms(x, g):
    return x * lax.rsqrt(jnp.mean(x * x, axis=-1, keepdims=True) + EPS) * g


def _pick(n, candidates):
    for c in candidates:
        if n % c == 0:
            return c
    return n


def _ada_kernel(c_ref, w_ref, b_ref, o_ref):
    s = c_ref[...]
    s = s * jax.nn.sigmoid(s)
    o_ref[...] = _dot(s.astype(BF16), w_ref[...].astype(BF16)) + b_ref[...]


def _ada(cc, w_mod, b_mod):
    depth, d, n = w_mod.shape
    tn = _pick(n, (1024, 512, 256, 128))
    return pl.pallas_call(
        _ada_kernel,
        grid=(depth, n // tn),
        in_specs=[pl.BlockSpec((SUBLANES, d), lambda l, j: (0, 0)),
                  pl.BlockSpec((None, d, tn), lambda l, j: (l, 0, j)),
                  pl.BlockSpec((None, 1, tn), lambda l, j: (l, 0, j))],
        out_specs=pl.BlockSpec((None, SUBLANES, tn), lambda l, j: (l, 0, j)),
        out_shape=jax.ShapeDtypeStruct((depth, SUBLANES, n), F32),
        compiler_params=_cparams(("arbitrary", "arbitrary")),
    )(cc, w_mod, b_mod.reshape(depth, 1, n))


def _mod3(mod_ref, sub, d):
    return tuple(mod_ref[:, (3 * sub + k) * d:(3 * sub + k + 1) * d] for k in range(3))


def _ffn_kernel(x_ref, mod_ref, g_ref, wg_ref, wu_ref, wd_ref, o_ref, h_ref, acc_ref, *, sub, d):
    j = pl.program_id(1)
    shift, scale, gate = _mod3(mod_ref, sub, d)

    @pl.when(j == 0)
    def _():
        h = _rms(x_ref[...], g_ref[2 * sub:2 * sub + 1, :]) * (1.0 + scale) + shift
        h_ref[...] = h.astype(BF16)
        acc_ref[...] = jnp.zeros_like(acc_ref)

    h = h_ref[...]
    a = _dot(h, wg_ref[...])
    u = _dot(h, wu_ref[...])
    act = (a * jax.nn.sigmoid(a) * u).astype(BF16)
    acc_ref[...] += _dot(act, wd_ref[...])

    @pl.when(j == pl.num_programs(1) - 1)
    def _():
        y = _rms(acc_ref[...], g_ref[2 * sub + 1:2 * sub + 2, :])
        o_ref[...] = x_ref[...] + 0.5 * gate * y


def _ffn(xt, mod, norm_g, wg, wu, wd, layer, sub, widx, lay, rows=None):
    d = xt.shape[1]
    r = xt.shape[0] if rows is None else rows
    f = wg.shape[-1]
    tm, tf = lay.tm, _pick(f, (512, 256, 128))
    return pl.pallas_call(
        functools.partial(_ffn_kernel, sub=sub, d=d),
        grid=(r // tm, f // tf),
        in_specs=[pl.BlockSpec((tm, d), lambda i, j: (i, 0)),
                  pl.BlockSpec((None, None, 1, N_MOD * d), lambda i, j: (layer, lay.mod_row(i), 0, 0)),
                  pl.BlockSpec((None, 6, d), lambda i, j: (layer, 0, 0)),
                  pl.BlockSpec((None, None, d, tf), lambda i, j: (layer, widx, 0, j)),
                  pl.BlockSpec((None, None, d, tf), lambda i, j: (layer, widx, 0, j)),
                  pl.BlockSpec((None, None, tf, d), lambda i, j: (layer, widx, j, 0))],
        out_specs=pl.BlockSpec((tm, d), lambda i, j: (i, 0)),
        out_shape=jax.ShapeDtypeStruct((r, d), F32),
        scratch_shapes=[pltpu.VMEM((tm, d), BF16), pltpu.VMEM((tm, d), F32)],
        compiler_params=_cparams(("parallel", "arbitrary")),
    )(xt, mod, norm_g, wg, wu, wd)


def _proj_kernel(x_ref, mod_ref, g_ref, w_ref, cos_ref, sin_ref, o_ref, h_ref, *, d, n_lat_tiles, n_rot_tiles,
                 n_q_tiles, q_scale):
    i, j = pl.program_id(0), pl.program_id(1)
    shift, scale, _ = _mod3(mod_ref, 1, d)

    @pl.when(j == 0)
    def _():
        h = _rms(x_ref[...], g_ref[2:3, :]) * (1.0 + scale) + shift
        h_ref[...] = h.astype(BF16)

    y = _dot(h_ref[...], w_ref[...])
    if n_q_tiles:
        y = y * jnp.where(j < n_q_tiles, q_scale, 1.0)
    rotate = jnp.logical_and(i < n_lat_tiles, j < n_rot_tiles)

    @pl.when(rotate)
    def _():
        tn = y.shape[1]
        lane = lax.broadcasted_iota(jnp.int32, (1, LANES), 1)
        first_half = (lane % 64) < 32
        for c in range(tn // LANES):
            yc = y[:, c * LANES:(c + 1) * LANES]
            partner = jnp.where(first_half, pltpu.roll(yc, LANES - 32, 1), pltpu.roll(yc, 32, 1))
            o_ref[:, c * LANES:(c + 1) * LANES] = (yc * cos_ref[...] + partner * sin_ref[...]).astype(o_ref.dtype)

    @pl.when(jnp.logical_not(rotate))
    def _():
        o_ref[...] = y.astype(o_ref.dtype)


def _proj(xt, mod, norm_g, w, widx, layer, lay, rope=None):
    r, d = xt.shape
    n = w.shape[-1]
    tm = lay.tm
    if rope is None:
        tn = _pick(n, (1024, 512, 256, 128))
        cos = sin = jnp.zeros((tm, LANES), F32)
        n_rot = n_q = 0
        tab_map = lambda i, j: (0, 0)
    else:
        cos, sin, n_rot, n_q = rope
        tn = _pick(math.gcd(n_q, n), (1024, 512, 256, 128))
        tab_map = lambda i, j: (i % lay.tiles_per_seq, 0)
    return pl.pallas_call(
        functools.partial(_proj_kernel, d=d, n_lat_tiles=lay.n_lat_tiles, n_rot_tiles=n_rot // tn,
                          n_q_tiles=n_q // tn, q_scale=lay.q_scale),
        grid=(r // tm, n // tn),
        in_specs=[pl.BlockSpec((tm, d), lambda i, j: (i, 0)),
                  pl.BlockSpec((None, None, 1, N_MOD * d), lambda i, j: (layer, lay.mod_row(i), 0, 0)),
                  pl.BlockSpec((None, 6, d), lambda i, j: (layer, 0, 0)),
                  pl.BlockSpec((None, d, tn), lambda i, j: (widx, 0, j)),
                  pl.BlockSpec((tm, LANES), tab_map),
                  pl.BlockSpec((tm, LANES), tab_map)],
        out_specs=pl.BlockSpec((tm, tn), lambda i, j: (i, j)),
        out_shape=jax.ShapeDtypeStruct((r, n), BF16),
        scratch_shapes=[pltpu.VMEM((tm, d), BF16)],
        compiler_params=_cparams(("parallel", "arbitrary")),
    )(xt, mod, norm_g, w, cos, sin)


def _rope_tables(n):
    rows = n // GRID_W
    row = jnp.repeat(jnp.arange(rows), GRID_W).astype(F32)
    col = jnp.tile(jnp.arange(GRID_W), rows).astype(F32)
    pairs = 16
    freqs = jnp.power(ROPE_BASE, -jnp.arange(pairs, dtype=F32) / pairs)
    ang = jnp.concatenate([row[:, None] * freqs, col[:, None] * freqs], axis=-1)
    cos, sin = jnp.cos(ang), jnp.sin(ang)
    return jnp.tile(cos, (1, 4)), jnp.concatenate([-sin, sin, -sin, sin], axis=-1)


def _s5_matrices(lam_re, lam_im, log_step, b_re, b_im, c_re, c_im, t):
    hp = lax.Precision.HIGHEST
    lam_re, lam_im = lam_re.astype(F32), lam_im.astype(F32)
    dt = jnp.exp(log_step.astype(F32))[..., None]
    lr, li = lam_re * dt, lam_im * dt
    mag = jnp.exp(lr)
    a_re, a_im = mag * jnp.cos(li), mag * jnp.sin(li)
    inv = 1.0 / (lam_re * lam_re + lam_im * lam_im)
    co_re = ((a_re - 1.0) * lam_re + a_im * lam_im) * inv
    co_im = (a_im * lam_re - (a_re - 1.0) * lam_im) * inv
    b_re, b_im = b_re.astype(F32), b_im.astype(F32)
    bb_re = co_re[..., None] * b_re - co_im[..., None] * b_im
    bb_im = co_re[..., None] * b_im + co_im[..., None] * b_re
    c_re, c_im = c_re.astype(F32), c_im.astype(F32)
    tau = jnp.arange(t + 1, dtype=F32)[:, None, None, None]
    pmag = jnp.exp(tau * lr[None])
    pw_re, pw_im = pmag * jnp.cos(tau * li[None]), pmag * jnp.sin(tau * li[None])
    cp_re = c_re[None] * pw_re[:, :, :, None, :] - c_im[None] * pw_im[:, :, :, None, :]
    cp_im = c_re[None] * pw_im[:, :, :, None, :] + c_im[None] * pw_re[:, :, :, None, :]
    kern = (jnp.einsum('tdghp,dgpk->tdghk', cp_re, bb_re, precision=hp)
            - jnp.einsum('tdghp,dgpk->tdghk', cp_im, bb_im, precision=hp))
    g, p, h = b_re.shape[1:]
    s_idx = jnp.arange(t)[:, None]
    t_idx = jnp.arange(t)[None, :]

    gb = S5_BLOCK
    nb = g // gb
    eye = jnp.eye(gb, dtype=F32)

    def toeplitz(k_dir, lag):
        m = jnp.where((lag >= 0)[:, :, None, None, None], k_dir[jnp.clip(lag, 0, t)], 0.0)
        return m.transpose(2, 0, 4, 1, 3).reshape(nb, gb, t, h, t, h).transpose(0, 2, 1, 3, 4, 5)

    mk = toeplitz(kern[:, 0], t_idx - s_idx) + toeplitz(kern[:, 1], s_idx - t_idx)
    mk = (mk[:, :, :, :, :, None, :] * eye[None, None, :, None, None, :, None]).reshape(nb, t * gb * h, t * gb * h)

    def drive(d, exps):
        wr, wi = pw_re[exps, d], pw_im[exps, d]
        re = wr[..., None] * bb_re[d][None] - wi[..., None] * bb_im[d][None]
        im = wr[..., None] * bb_im[d][None] + wi[..., None] * bb_re[d][None]

        def f(m):
            m = m.reshape(t, nb, gb, p, h).transpose(1, 0, 2, 4, 3)
            return (m[:, :, :, :, None, :] * eye[None, None, :, None, :, None]).reshape(nb, t * gb * h, gb * p)
        return jnp.concatenate([f(re), f(im)], axis=-1).astype(BF16)

    def read(d, exps):
        def f(m):
            m = m.reshape(t, nb, gb, h, p).transpose(1, 2, 4, 0, 3)
            return (m[:, :, :, :, None, :] * eye[None, :, None, None, :, None]).reshape(nb, gb * p, t * gb * h)
        return jnp.concatenate([f(cp_re[exps, d]), f(-cp_im[exps, d])], axis=1).astype(BF16)

    ar = jnp.arange(t)
    ms_f, ms_r = drive(0, t - 1 - ar), drive(1, ar)
    mo_f, mo_r = read(0, ar + 1), read(1, t - ar)
    at = jnp.stack([pw_re[t, 0], pw_im[t, 0], pw_re[t, 1], pw_im[t, 1]], axis=1)
    at = at.reshape(nb, gb, 4, p).transpose(0, 2, 1, 3).reshape(nb, 4, gb * p)
    return mk.astype(BF16), ms_f, ms_r, mo_f, mo_r, at


def _s5_kernel(uf_ref, mk_ref, msf_ref, msr_ref, mof_ref, mor_ref, a_ref, y_ref, sf_ref, sr_ref, hf_ref, hr_ref, *,
               batch, lat_tiles, ctx_tiles):
    w = a_ref.shape[-1]
    uf = uf_ref[...]
    sf_ref[...] = _dot(uf, msf_ref[...])
    sr_ref[...] = _dot(uf, msr_ref[...])
    a = a_ref[...]
    af = (a[0:1], a[1:2])
    ab = (a[2:3], a[3:4])
    row_id = lax.broadcasted_iota(jnp.int32, (SUBLANES, 2 * w), 0)

    def tile_step(s_ref, hp_ref, tile, state, coef, rows):
        off = pl.multiple_of(tile * SUBLANES, SUBLANES)
        s = s_ref[pl.ds(off, SUBLANES), :]
        hre, him = state
        are, aim = coef
        prev = jnp.zeros((SUBLANES, 2 * w), F32)
        for i in rows:
            row = jnp.concatenate([hre, him], axis=1)
            prev = jnp.where(row_id == i, jnp.broadcast_to(row, prev.shape), prev)
            sre, sim = s[i:i + 1, :w], s[i:i + 1, w:]
            hre, him = are * hre - aim * him + sre, are * him + aim * hre + sim
        hp_ref[pl.ds(off, SUBLANES), :] = prev
        return hre, him

    zero = jnp.zeros((1, w), F32)
    state = tuple((zero, zero) for _ in range(2 * batch))
    asc, desc = tuple(range(SUBLANES)), tuple(reversed(range(SUBLANES)))

    def phase(first_tile, n_tiles, state):
        def body(k, st):
            out = []
            for b in range(batch):
                out.append(tile_step(sf_ref, hf_ref, first_tile(b) + k, st[2 * b], af, asc))
                out.append(tile_step(sr_ref, hr_ref, first_tile(b) + n_tiles - 1 - k, st[2 * b + 1], ab, desc))
            return tuple(out)
        return lax.fori_loop(0, n_tiles, body, state)

    state = phase(lambda b: batch * lat_tiles + b * ctx_tiles, ctx_tiles, state)
    phase(lambda b: b * lat_tiles, lat_tiles, state)

    y_ref[...] = (_dot(uf, mk_ref[...]) + _dot(hf_ref[...].astype(BF16), mof_ref[...])
                  + _dot(hr_ref[...].astype(BF16), mor_ref[...]))


def _s5(uf, mats, lay):
    mk, ms_f, ms_r, mo_f, mo_r, at = mats
    nb, nc, wu = uf.shape
    ws = ms_f.shape[-1]
    t = S5_CHUNK
    full = lambda *shape: pl.BlockSpec((None,) + shape, lambda q: (q,) + (0,) * len(shape))
    return pl.pallas_call(
        functools.partial(_s5_kernel, batch=lay.batch, lat_tiles=lay.seq // (t * SUBLANES),
                          ctx_tiles=lay.ctx // (t * SUBLANES)),
        grid=(nb,),
        in_specs=[full(nc, wu), full(wu, wu), full(wu, ws), full(wu, ws), full(ws, wu), full(ws, wu),
                  full(4, ws // 2)],
        out_specs=full(nc, wu),
        out_shape=jax.ShapeDtypeStruct((nb, nc, wu), F32),
        scratch_shapes=[pltpu.VMEM((nc, ws), F32)] * 4,
        compiler_params=_cparams(("parallel",)),
    )(uf, mk, ms_f, ms_r, mo_f, mo_r, at)


def _gelu_tanh(x):
    return 0.5 * x * (1.0 + jnp.tanh(math.sqrt(2.0 / math.pi) * (x + 0.044715 * (x * x * x))))


def _mixout_kernel(x_ref, mod_ref, g_ref, y_ref, u_ref, bg_ref, cg_ref, v_ref, cgp_ref, vp_ref, cgn_ref, vn_ref,
                   kp_ref, kn_ref, d_ref, wglu_ref, bglu_ref, cw_ref, cb_ref, wout_ref, o_ref, *, d, halo):
    _, _, gate = _mod3(mod_ref, 1, d)
    sw = u_ref.shape[1]
    tm = x_ref.shape[0]
    z = _gelu_tanh(y_ref[...] + d_ref[...] * u_ref[...].astype(F32))
    s5 = z * jax.nn.sigmoid(_dot(z.astype(BF16), wglu_ref[...]) + bglu_ref[...])
    gv = cg_ref[...].astype(F32) * v_ref[...].astype(F32)
    gv_before = cgp_ref[halo - 1:halo, :].astype(F32) * vp_ref[halo - 1:halo, :].astype(F32)
    gv_after = cgn_ref[0:1, :].astype(F32) * vn_ref[0:1, :].astype(F32)
    row = lax.broadcasted_iota(jnp.int32, (tm, 1), 0)
    prev = jnp.where(row == 0, gv_before, pltpu.roll(gv, 1, 0)) * kp_ref[...]
    nxt = jnp.where(row == tm - 1, gv_after, pltpu.roll(gv, tm - 1, 0)) * kn_ref[...]
    conv = cw_ref[0:1, :] * prev + cw_ref[1:2, :] * gv + cw_ref[2:3, :] * nxt + cb_ref[...]
    conv = bg_ref[...].astype(F32) * conv
    y = _dot(s5.astype(BF16), wout_ref[:sw, :]) + _dot(conv.astype(BF16), wout_ref[sw:, :])
    o_ref[...] = x_ref[...] + gate * _rms(y, g_ref[3:4, :])


def _mixout(xt, mod, norm_g, ys, p, keep_prev, keep_next, s5_d, w_glu, b_glu, conv_w, conv_b, w_out, layer, i_even,
            lay):
    r, d = xt.shape
    sw = ys.shape[1]
    tm = min(lay.tm, 256)
    halo = 16
    hb = tm // halo
    last_halo = r // halo - 1
    col = lambda c: pl.BlockSpec((tm, sw), lambda i: (i, c))
    before = lambda c: pl.BlockSpec((halo, sw), lambda i: (jnp.maximum(i * hb - 1, 0), c))
    after = lambda c: pl.BlockSpec((halo, sw), lambda i: (jnp.minimum((i + 1) * hb, last_halo), c))
    vec = lambda n: pl.BlockSpec((None, 1, n), lambda i: (i_even, 0, 0))
    lay_m = lay.with_tm(tm)
    return pl.pallas_call(
        functools.partial(_mixout_kernel, d=d, halo=halo),
        grid=(r // tm,),
        in_specs=[pl.BlockSpec((tm, d), lambda i: (i, 0)),
                  pl.BlockSpec((None, None, 1, N_MOD * d), lambda i: (layer, lay_m.mod_row(i), 0, 0)),
                  pl.BlockSpec((None, 6, d), lambda i: (layer, 0, 0)),
                  pl.BlockSpec((tm, sw), lambda i: (i, 0)),
                  col(0), col(1), col(2), col(3), before(2), before(3), after(2), after(3),
                  pl.BlockSpec((tm, 1), lambda i: (i, 0)),
                  pl.BlockSpec((tm, 1), lambda i: (i, 0)),
                  vec(sw),
                  pl.BlockSpec((None, sw, sw), lambda i: (i_even, 0, 0)),
                  vec(sw),
                  pl.BlockSpec((None, 3, sw), lambda i: (i_even, 0, 0)),
                  vec(sw),
                  pl.BlockSpec((None, 2 * sw, d), lambda i: (i_even, 0, 0))],
        out_specs=pl.BlockSpec((tm, d), lambda i: (i, 0)),
        out_shape=jax.ShapeDtypeStruct((r, d), F32),
        compiler_params=_cparams(("parallel",)),
    )(xt, mod, norm_g, ys, p, p, p, p, p, p, p, p, keep_prev, keep_next,
      s5_d.reshape(s5_d.shape[0], 1, sw), w_glu, b_glu.reshape(b_glu.shape[0], 1, sw), conv_w,
      conv_b.reshape(conv_b.shape[0], 1, sw), w_out)


def _attn_kernel(lam_ref, qt_ref, kc_ref, vct_ref, k_ref, vt_ref, sub_ref, o_ref, q2_ref, s_ref, m_ref, al_ref, l_ref,
                 acc_ref, *, n_lat_q, tk, out_scale):
    qi = pl.program_id(2)
    hd, tq = qt_ref.shape
    qt = qt_ref[...]
    row = lax.broadcasted_iota(jnp.int32, (hd, tq), 0)
    zero = jnp.zeros_like(qt)
    q2_ref[:, :tq] = jnp.where(row < hd // 2, qt, zero)
    q2_ref[:, tq:] = jnp.where(row >= hd // 2, qt, zero)

    s = _dot(kc_ref[...], q2_ref[...])
    m0 = jnp.max(s, axis=0, keepdims=True)
    pr = jnp.exp2(s - m0)
    m_ref[...] = m0
    l_ref[...] = jnp.sum(pr, axis=0, keepdims=True)
    acc_ref[...] = _dot(vct_ref[...], pr.astype(BF16))

    @pl.when(qi < n_lat_q)
    def _():
        n_keys = k_ref.shape[0]

        def scores(off):
            return _dot(k_ref[pl.ds(pl.multiple_of(off, tk), tk), :], q2_ref[...])

        def fold_max(buf):
            m_prev = m_ref[...]
            m_new = jnp.maximum(m_prev, jnp.max(s_ref[buf], axis=0, keepdims=True))
            al_ref[...] = jnp.exp2(m_prev - m_new)
            m_ref[...] = m_new

        def stage(cur, nxt, off_cur, off_nxt):
            s_ref[nxt] = scores(off_nxt)
            alpha = al_ref[...]
            pr = jnp.exp2(s_ref[cur] - m_ref[...])
            l_ref[...] = alpha * l_ref[...] + jnp.sum(pr, axis=0, keepdims=True)
            pv = _dot(vt_ref[:, pl.ds(pl.multiple_of(off_cur, tk), tk)], pr.astype(BF16))
            fold_max(nxt)
            acc_ref[...] = alpha * acc_ref[...] + pv

        s_ref[0] = scores(0)
        fold_max(0)

        def body(c, carry):
            off = ATTN_UNROLL * c * tk
            for u in range(ATTN_UNROLL):
                stage(u % 2, (u + 1) % 2, off + u * tk, jnp.minimum(off + (u + 1) * tk, n_keys - tk))
            return carry
        lax.fori_loop(0, n_keys // (ATTN_UNROLL * tk), body, 0)

    o = acc_ref[...] * (1.0 / l_ref[...])
    o = o[:, :tq] - lam_ref[0] * o[:, tq:]
    o = o * lax.rsqrt(jnp.mean(o * o, axis=0, keepdims=True) + EPS)
    o = o * jnp.tile(sub_ref[...], (1, tq // LANES)) * out_scale
    o_ref[...] = o.T.astype(o_ref.dtype)


def _attn(qkv, lam, subln, lam_init, lay):
    r = qkv.shape[0]
    hd = subln.shape[-1]
    heads = qkv.shape[1] // (3 * hd)
    d = heads * hd
    tq = lay.ctx
    tk = _pick(lay.seq, (2048, 1024, 512)) // ATTN_UNROLL
    n_lat_q = lay.seq // tq
    lat_blocks = lay.batch * n_lat_q
    qt = qkv[:, :d].T
    vt = qkv[:, 2 * d:].T

    def q_block(b, qi):
        return jnp.where(qi < n_lat_q, b * n_lat_q + qi, lat_blocks + b)

    return pl.pallas_call(
        functools.partial(_attn_kernel, n_lat_q=n_lat_q, tk=tk, out_scale=1.0 - lam_init),
        grid=(lay.batch, heads, n_lat_q + 1),
        in_specs=[pl.BlockSpec(memory_space=pltpu.SMEM),
                  pl.BlockSpec((hd, tq), lambda b, h, qi: (h, q_block(b, qi))),
                  pl.BlockSpec((lay.ctx, hd), lambda b, h, qi: (lat_blocks + b, heads + h)),
                  pl.BlockSpec((hd, lay.ctx), lambda b, h, qi: (h, lat_blocks + b)),
                  pl.BlockSpec((lay.seq, hd), lambda b, h, qi: (b, heads + h)),
                  pl.BlockSpec((hd, lay.seq), lambda b, h, qi: (h, b)),
                  pl.BlockSpec((hd, LANES), lambda b, h, qi: (0, 0))],
        out_specs=pl.BlockSpec((tq, hd), lambda b, h, qi: (q_block(b, qi), h)),
        out_shape=jax.ShapeDtypeStruct((r, d), BF16),
        scratch_shapes=[pltpu.VMEM((hd, 2 * tq), BF16), pltpu.VMEM((2, tk, 2 * tq), F32)]
        + [pltpu.VMEM((1, 2 * tq), F32)] * 3 + [pltpu.VMEM((hd, 2 * tq), F32)],
        compiler_params=_cparams(("parallel", "parallel", "arbitrary")),
    )(lam, qt, qkv, vt, qkv, vt, jnp.broadcast_to(subln.reshape(hd, 1), (hd, LANES)))


def _attnout_kernel(x_ref, mod_ref, g_ref, a_ref, w_ref, o_ref, *, d):
    _, _, gate = _mod3(mod_ref, 1, d)
    y = _dot(a_ref[...], w_ref[...])
    o_ref[...] = x_ref[...] + gate * _rms(y, g_ref[3:4, :])


def _attnout(xt, mod, norm_g, a, w_o, layer, i_odd, lay):
    r, d = xt.shape
    tm = min(lay.tm, 256)
    lay_m = lay.with_tm(tm)
    return pl.pallas_call(
        functools.partial(_attnout_kernel, d=d),
        grid=(r // tm,),
        in_specs=[pl.BlockSpec((tm, d), lambda i: (i, 0)),
                  pl.BlockSpec((None, None, 1, N_MOD * d), lambda i: (layer, lay_m.mod_row(i), 0, 0)),
                  pl.BlockSpec((None, 6, d), lambda i: (layer, 0, 0)),
                  pl.BlockSpec((tm, a.shape[1]), lambda i: (i, 0)),
                  pl.BlockSpec((None, a.shape[1], d), lambda i: (i_odd, 0, 0))],
        out_specs=pl.BlockSpec((tm, d), lambda i: (i, 0)),
        out_shape=jax.ShapeDtypeStruct((r, d), F32),
        compiler_params=_cparams(("parallel",)),
    )(xt, mod, norm_g, a, w_o)


class _Layout:
    def __init__(self, batch, seq, ctx, tm, q_scale):
        self.batch, self.seq, self.ctx, self.tm, self.q_scale = batch, seq, ctx, tm, q_scale
        self.tiles_per_seq = seq // tm
        self.n_lat_tiles = batch * self.tiles_per_seq

    def with_tm(self, tm):
        return _Layout(self.batch, self.seq, self.ctx, tm, self.q_scale)

    def mod_row(self, i):
        return jnp.minimum(i // self.tiles_per_seq, self.batch)


def kernel(x, c, ctx, c_ctx, w_mod, b_mod, norm_g, ffn_wg, ffn_wu, ffn_wd, mix_w_in, mix_w_out, s5_lam_re, s5_lam_im, s5_log_step, s5_b_re, s5_b_im, s5_c_re, s5_c_im, s5_d, s5_w_glu, s5_b_glu, conv_w, conv_b, attn_w_qkv, attn_w_o, attn_lambda, attn_subln):
    batch, seq, d = x.shape
    n_ctx = ctx.shape[1]
    depth = w_mod.shape[0]
    hd = attn_subln.shape[-1]
    dh = attn_lambda.shape[-1]
    g, p, h = s5_b_re.shape[2:]
    sw = g * h
    t = S5_CHUNK
    assert batch + 1 <= SUBLANES and conv_w.shape[-1] == sw and mix_w_in.shape[-1] == 4 * sw
    assert g % S5_BLOCK == 0 and (S5_BLOCK * p) % LANES == 0 and (S5_BLOCK * h * t) % LANES == 0
    assert seq % (t * SUBLANES) == 0 and n_ctx % (t * SUBLANES) == 0 and seq % n_ctx == 0 and hd == 2 * dh
    tm = _pick(math.gcd(seq, batch * n_ctx), (512, 256, 128, 64, 32, 16))
    lay = _Layout(batch, seq, n_ctx, tm, dh ** -0.5 * math.log2(math.e))
    r = batch * (seq + n_ctx)

    xt = jnp.concatenate([x.reshape(batch * seq, d), ctx.reshape(batch * n_ctx, d)], axis=0)
    cc = jnp.zeros((SUBLANES, d), F32).at[:batch].set(c).at[batch].set(c_ctx)
    mod = _ada(cc, w_mod, b_mod).reshape(depth, SUBLANES, 1, N_MOD * d)

    wg, wu, wd = ffn_wg.astype(BF16), ffn_wu.astype(BF16), ffn_wd.astype(BF16)
    w_in, w_out, w_glu = mix_w_in.astype(BF16), mix_w_out.astype(BF16), s5_w_glu.astype(BF16)
    w_qkv, w_o = attn_w_qkv.astype(BF16), attn_w_o.astype(BF16)

    cos, sin = _rope_tables(seq)
    pos = jnp.concatenate([jnp.tile(jnp.arange(seq), batch), jnp.tile(jnp.arange(n_ctx), batch)])
    last = jnp.concatenate([jnp.full((batch * seq,), seq - 1), jnp.full((batch * n_ctx,), n_ctx - 1)])
    keep_prev = (pos != 0).astype(F32)[:, None]
    keep_next = (pos != last).astype(F32)[:, None]
    nc = r // t

    for layer in range(depth):
        i = layer // 2
        xt = _ffn(xt, mod, norm_g, wg, wu, wd, layer, 0, 0, lay)
        if layer % 2 == 0:
            pj = _proj(xt, mod, norm_g, w_in, i, layer, lay)
            mats = _s5_matrices(s5_lam_re[i], s5_lam_im[i], s5_log_step[i], s5_b_re[i], s5_b_im[i],
                                s5_c_re[i], s5_c_im[i], t)
            nb, wb = g // S5_BLOCK, S5_BLOCK * h
            uf = pj[:, :sw].reshape(nc, t, nb, wb).transpose(2, 0, 1, 3).reshape(nb, nc, t * wb)
            ys = _s5(uf, mats, lay)
            ys = ys.reshape(nb, nc, t, wb).transpose(1, 2, 0, 3).reshape(r, sw)
            xt = _mixout(xt, mod, norm_g, ys, pj, keep_prev, keep_next, s5_d, w_glu, s5_b_glu, conv_w, conv_b, w_out,
                         layer, i, lay)
        else:
            lam_init = 0.8 - 0.6 * math.exp(-0.3 * layer)
            lv = attn_lambda[i].astype(F32)
            lam = (jnp.exp(jnp.sum(lv[0] * lv[1])) - jnp.exp(jnp.sum(lv[2] * lv[3])) + lam_init).reshape(1)
            qkv = _proj(xt, mod, norm_g, w_qkv, i, layer, lay, rope=(cos, sin, 2 * d, d))
            a = _attn(qkv, lam, attn_subln[i], lam_init, lay)
            xt = _attnout(xt, mod, norm_g, a, w_o, layer, i, lay)
        xt = _ffn(xt, mod, norm_g, wg, wu, wd, layer, 2, 1, lay, rows=batch * seq if layer == depth - 1 else None)
    return xt.reshape(batch, seq, d)
```

```python
import functools
import math

import jax
import jax.numpy as jnp
from jax import lax
from jax.experimental import pallas as pl
from jax.experimental.pallas import tpu as pltpu

F32 = jnp.float32
BF16 = jnp.bfloat16
EPS = 1e-6
N_MOD = 9
GRID_W = 64
ROPE_BASE = 10000.0
S5_CHUNK = 16
S5_BLOCK = 4
SUBLANES = 8
LANES = 128
VMEM_LIMIT = 56 * 1024 * 1024


def _cparams(sem):
    return pltpu.CompilerParams(dimension_semantics=sem, vmem_limit_bytes=VMEM_LIMIT)


def _dot(a, b):
    return jnp.dot(a, b, preferred_element_type=F32)


def _rms(x, g):
    return x * lax.rsqrt(jnp.mean(x * x, axis=-1, keepdims=True) + EPS) * g


def _pick(n, candidates):
    for c in candidates:
        if n % c == 0:
            return c
    return n


def _ada_kernel(c_ref, w_ref, b_ref, o_ref):
    s = c_ref[...]
    s = s * jax.nn.sigmoid(s)
    o_ref[...] = _dot(s.astype(BF16), w_ref[...].astype(BF16)) + b_ref[...]


def _ada(cc, w_mod, b_mod):
    depth, d, n = w_mod.shape
    tn = _pick(n, (1024, 512, 256, 128))
    return pl.pallas_call(
        _ada_kernel,
        grid=(depth, n // tn),
        in_specs=[pl.BlockSpec((SUBLANES, d), lambda l, j: (0, 0)),
                  pl.BlockSpec((None, d, tn), lambda l, j: (l, 0, j)),
                  pl.BlockSpec((None, 1, tn), lambda l, j: (l, 0, j))],
        out_specs=pl.BlockSpec((None, SUBLANES, tn), lambda l, j: (l, 0, j)),
        out_shape=jax.ShapeDtypeStruct((depth, SUBLANES, n), F32),
        compiler_params=_cparams(("arbitrary", "arbitrary")),
    )(cc, w_mod, b_mod.reshape(depth, 1, n))


def _mod3(mod_ref, sub, d):
    return tuple(mod_ref[:, (3 * sub + k) * d:(3 * sub + k + 1) * d] for k in range(3))


def _ffn_kernel(x_ref, mod_ref, g_ref, wg_ref, wu_ref, wd_ref, o_ref, h_ref, acc_ref, *, sub, d):
    j = pl.program_id(1)
    shift, scale, gate = _mod3(mod_ref, sub, d)

    @pl.when(j == 0)
    def _():
        h = _rms(x_ref[...], g_ref[2 * sub:2 * sub + 1, :]) * (1.0 + scale) + shift
        h_ref[...] = h.astype(BF16)
        acc_ref[...] = jnp.zeros_like(acc_ref)

    h = h_ref[...]
    a = _dot(h, wg_ref[...])
    u = _dot(h, wu_ref[...])
    act = (a * jax.nn.sigmoid(a) * u).astype(BF16)
    acc_ref[...] += _dot(act, wd_ref[...])

    @pl.when(j == pl.num_programs(1) - 1)
    def _():
        y = _rms(acc_ref[...], g_ref[2 * sub + 1:2 * sub + 2, :])
        o_ref[...] = x_ref[...] + 0.5 * gate * y


def _ffn(xt, mod, norm_g, wg, wu, wd, layer, sub, widx, lay, rows=None):
    d = xt.shape[1]
    r = xt.shape[0] if rows is None else rows
    f = wg.shape[-1]
    tm, tf = lay.tm, _pick(f, (512, 256, 128))
    return pl.pallas_call(
        functools.partial(_ffn_kernel, sub=sub, d=d),
        grid=(r // tm, f // tf),
        in_specs=[pl.BlockSpec((tm, d), lambda i, j: (i, 0)),
                  pl.BlockSpec((None, None, 1, N_MOD * d), lambda i, j: (layer, lay.mod_row(i), 0, 0)),
                  pl.BlockSpec((None, 6, d), lambda i, j: (layer, 0, 0)),
                  pl.BlockSpec((None, None, d, tf), lambda i, j: (layer, widx, 0, j)),
                  pl.BlockSpec((None, None, d, tf), lambda i, j: (layer, widx, 0, j)),
                  pl.BlockSpec((None, None, tf, d), lambda i, j: (layer, widx, j, 0))],
        out_specs=pl.BlockSpec((tm, d), lambda i, j: (i, 0)),
        out_shape=jax.ShapeDtypeStruct((r, d), F32),
        scratch_shapes=[pltpu.VMEM((tm, d), BF16), pltpu.VMEM((tm, d), F32)],
        compiler_params=_cparams(("parallel", "arbitrary")),
    )(xt, mod, norm_g, wg, wu, wd)


def _proj_kernel(x_ref, mod_ref, g_ref, w_ref, cos_ref, sin_ref, o_ref, h_ref, *, d, n_lat_tiles, n_rot_tiles,
                 n_q_tiles, q_scale):
    i, j = pl.program_id(0), pl.program_id(1)
    shift, scale, _ = _mod3(mod_ref, 1, d)

    @pl.when(j == 0)
    def _():
        h = _rms(x_ref[...], g_ref[2:3, :]) * (1.0 + scale) + shift
        h_ref[...] = h.astype(BF16)

    y = _dot(h_ref[...], w_ref[...])
    if n_q_tiles:
        y = y * jnp.where(j < n_q_tiles, q_scale, 1.0)
    rotate = jnp.logical_and(i < n_lat_tiles, j < n_rot_tiles)

    @pl.when(rotate)
    def _():
        tn = y.shape[1]
        lane = lax.broadcasted_iota(jnp.int32, (1, LANES), 1)
        first_half = (lane % 64) < 32
        for c in range(tn // LANES):
            yc = y[:, c * LANES:(c + 1) * LANES]
            partner = jnp.where(first_half, pltpu.roll(yc, LANES - 32, 1), pltpu.roll(yc, 32, 1))
            o_ref[:, c * LANES:(c + 1) * LANES] = (yc * cos_ref[...] + partner * sin_ref[...]).astype(o_ref.dtype)

    @pl.when(jnp.logical_not(rotate))
    def _():
        o_ref[...] = y.astype(o_ref.dtype)


def _proj(xt, mod, norm_g, w, widx, layer, lay, rope=None):
    r, d = xt.shape
    n = w.shape[-1]
    tm = lay.tm
    if rope is None:
        tn = _pick(n, (1024, 512, 256, 128))
        cos = sin = jnp.zeros((tm, LANES), F32)
        n_rot = n_q = 0
        tab_map = lambda i, j: (0, 0)
    else:
        cos, sin, n_rot, n_q = rope
        tn = _pick(math.gcd(n_q, n), (1024, 512, 256, 128))
        tab_map = lambda i, j: (i % lay.tiles_per_seq, 0)
    return pl.pallas_call(
        functools.partial(_proj_kernel, d=d, n_lat_tiles=lay.n_lat_tiles, n_rot_tiles=n_rot // tn,
                          n_q_tiles=n_q // tn, q_scale=lay.q_scale),
        grid=(r // tm, n // tn),
        in_specs=[pl.BlockSpec((tm, d), lambda i, j: (i, 0)),
                  pl.BlockSpec((None, None, 1, N_MOD * d), lambda i, j: (layer, lay.mod_row(i), 0, 0)),
                  pl.BlockSpec((None, 6, d), lambda i, j: (layer, 0, 0)),
                  pl.BlockSpec((None, d, tn), lambda i, j: (widx, 0, j)),
                  pl.BlockSpec((tm, LANES), tab_map),
                  pl.BlockSpec((tm, LANES), tab_map)],
        out_specs=pl.BlockSpec((tm, tn), lambda i, j: (i, j)),
        out_shape=jax.ShapeDtypeStruct((r, n), BF16),
        scratch_shapes=[pltpu.VMEM((tm, d), BF16)],
        compiler_params=_cparams(("parallel", "arbitrary")),
    )(xt, mod, norm_g, w, cos, sin)


def _rope_tables(n):
    rows = n // GRID_W
    row = jnp.repeat(jnp.arange(rows), GRID_W).astype(F32)
    col = jnp.tile(jnp.arange(GRID_W), rows).astype(F32)
    pairs = 16
    freqs = jnp.power(ROPE_BASE, -jnp.arange(pairs, dtype=F32) / pairs)
    ang = jnp.concatenate([row[:, None] * freqs, col[:, None] * freqs], axis=-1)
    cos, sin = jnp.cos(ang), jnp.sin(ang)
    return jnp.tile(cos, (1, 4)), jnp.concatenate([-sin, sin, -sin, sin], axis=-1)


def _s5_matrices(lam_re, lam_im, log_step, b_re, b_im, c_re, c_im, t):
    hp = lax.Precision.HIGHEST
    lam_re, lam_im = lam_re.astype(F32), lam_im.astype(F32)
    dt = jnp.exp(log_step.astype(F32))[..., None]
    lr, li = lam_re * dt, lam_im * dt
    mag = jnp.exp(lr)
    a_re, a_im = mag * jnp.cos(li), mag * jnp.sin(li)
    inv = 1.0 / (lam_re * lam_re + lam_im * lam_im)
    co_re = ((a_re - 1.0) * lam_re + a_im * lam_im) * inv
    co_im = (a_im * lam_re - (a_re - 1.0) * lam_im) * inv
    b_re, b_im = b_re.astype(F32), b_im.astype(F32)
    bb_re = co_re[..., None] * b_re - co_im[..., None] * b_im
    bb_im = co_re[..., None] * b_im + co_im[..., None] * b_re
    c_re, c_im = c_re.astype(F32), c_im.astype(F32)
    tau = jnp.arange(t + 1, dtype=F32)[:, None, None, None]
    pmag = jnp.exp(tau * lr[None])
    pw_re, pw_im = pmag * jnp.cos(tau * li[None]), pmag * jnp.sin(tau * li[None])
    cp_re = c_re[None] * pw_re[:, :, :, None, :] - c_im[None] * pw_im[:, :, :, None, :]
    cp_im = c_re[None] * pw_im[:, :, :, None, :] + c_im[None] * pw_re[:, :, :, None, :]
    kern = (jnp.einsum('tdghp,dgpk->tdghk', cp_re, bb_re, precision=hp)
            - jnp.einsum('tdghp,dgpk->tdghk', cp_im, bb_im, precision=hp))
    g, p, h = b_re.shape[1:]
    s_idx = jnp.arange(t)[:, None]
    t_idx = jnp.arange(t)[None, :]

    gb = S5_BLOCK
    nb = g // gb
    eye = jnp.eye(gb, dtype=BF16)

    def toeplitz(k_dir, lag):
        m = jnp.where((lag >= 0)[:, :, None, None, None], k_dir[jnp.clip(lag, 0, t)], 0.0)
        return m.transpose(2, 0, 4, 1, 3).reshape(nb, gb, t, h, t, h).transpose(0, 2, 1, 3, 4, 5)

    mk = toeplitz(kern[:, 0], t_idx - s_idx) + toeplitz(kern[:, 1], s_idx - t_idx)
    mk = mk.astype(BF16)[:, :, :, :, :, None, :] * eye[None, None, :, None, None, :, None]
    mk = mk.reshape(nb, t * gb * h, t * gb * h)

    def drive(d, exps):
        wr, wi = pw_re[exps, d], pw_im[exps, d]
        re = wr[..., None] * bb_re[d][None] - wi[..., None] * bb_im[d][None]
        im = wr[..., None] * bb_im[d][None] + wi[..., None] * bb_re[d][None]

        def f(m):
            m = m.reshape(t, nb, gb, p, h).transpose(1, 0, 2, 4, 3).astype(BF16)
            return (m[:, :, :, :, None, :] * eye[None, None, :, None, :, None]).reshape(nb, t * gb * h, gb * p)
        return jnp.concatenate([f(re), f(im)], axis=-1).astype(BF16)

    def read(d, exps):
        def f(m):
            m = m.reshape(t, nb, gb, h, p).transpose(1, 2, 4, 0, 3).astype(BF16)
            return (m[:, :, :, :, None, :] * eye[None, :, None, None, :, None]).reshape(nb, gb * p, t * gb * h)
        return jnp.concatenate([f(cp_re[exps, d]), f(-cp_im[exps, d])], axis=1).astype(BF16)

    ar = jnp.arange(t)
    ms_f, ms_r = drive(0, t - 1 - ar), drive(1, ar)
    mo_f, mo_r = read(0, ar + 1), read(1, t - ar)
    at = jnp.stack([pw_re[t, 0], pw_im[t, 0], pw_re[t, 1], pw_im[t, 1]], axis=1)
    at = at.reshape(nb, gb, 4, p).transpose(0, 2, 1, 3).reshape(nb, 4, gb * p)
    return mk.astype(BF16), ms_f, ms_r, mo_f, mo_r, at


def _s5_kernel(uf_ref, mk_ref, msf_ref, msr_ref, mof_ref, mor_ref, a_ref, y_ref, sf_ref, sr_ref, hf_ref, hr_ref, *,
               batch, lat_tiles, ctx_tiles):
    w = a_ref.shape[-1]
    uf = uf_ref[...]
    sf_ref[...] = _dot(uf, msf_ref[...])
    sr_ref[...] = _dot(uf, msr_ref[...])
    a = a_ref[...]
    af = (a[0:1], a[1:2])
    ab = (a[2:3], a[3:4])
    row_id = lax.broadcasted_iota(jnp.int32, (SUBLANES, 2 * w), 0)

    def tile_step(s_ref, hp_ref, tile, state, coef, rows):
        off = pl.multiple_of(tile * SUBLANES, SUBLANES)
        s = s_ref[pl.ds(off, SUBLANES), :]
        hre, him = state
        are, aim = coef
        prev = jnp.zeros((SUBLANES, 2 * w), F32)
        for i in rows:
            row = jnp.concatenate([hre, him], axis=1)
            prev = jnp.where(row_id == i, jnp.broadcast_to(row, prev.shape), prev)
            sre, sim = s[i:i + 1, :w], s[i:i + 1, w:]
            hre, him = are * hre - aim * him + sre, are * him + aim * hre + sim
        hp_ref[pl.ds(off, SUBLANES), :] = prev
        return hre, him

    zero = jnp.zeros((1, w), F32)
    state = tuple((zero, zero) for _ in range(2 * batch))
    asc, desc = tuple(range(SUBLANES)), tuple(reversed(range(SUBLANES)))

    def phase(first_tile, n_tiles, state):
        def body(k, st):
            out = []
            for b in range(batch):
                out.append(tile_step(sf_ref, hf_ref, first_tile(b) + k, st[2 * b], af, asc))
                out.append(tile_step(sr_ref, hr_ref, first_tile(b) + n_tiles - 1 - k, st[2 * b + 1], ab, desc))
            return tuple(out)
        return lax.fori_loop(0, n_tiles, body, state)

    state = phase(lambda b: batch * lat_tiles + b * ctx_tiles, ctx_tiles, state)
    phase(lambda b: b * lat_tiles, lat_tiles, state)

    y_ref[...] = (_dot(uf, mk_ref[...]) + _dot(hf_ref[...].astype(BF16), mof_ref[...])
                  + _dot(hr_ref[...].astype(BF16), mor_ref[...]))


def _s5(uf, mats, i_even, lay):
    mk, ms_f, ms_r, mo_f, mo_r, at = mats
    nb, nc, wu = uf.shape
    ws = ms_f.shape[-1]
    t = S5_CHUNK
    full = lambda *shape: pl.BlockSpec((None,) + shape, lambda q: (q,) + (0,) * len(shape))
    mat = lambda *shape: pl.BlockSpec((None, None) + shape, lambda q: (i_even, q) + (0,) * len(shape))
    return pl.pallas_call(
        functools.partial(_s5_kernel, batch=lay.batch, lat_tiles=lay.seq // (t * SUBLANES),
                          ctx_tiles=lay.ctx // (t * SUBLANES)),
        grid=(nb,),
        in_specs=[full(nc, wu), mat(wu, wu), mat(wu, ws), mat(wu, ws), mat(ws, wu), mat(ws, wu),
                  mat(4, ws // 2)],
        out_specs=full(nc, wu),
        out_shape=jax.ShapeDtypeStruct((nb, nc, wu), F32),
        scratch_shapes=[pltpu.VMEM((nc, ws), F32)] * 4,
        compiler_params=_cparams(("parallel",)),
    )(uf, mk, ms_f, ms_r, mo_f, mo_r, at)


def _gelu_tanh(x):
    return 0.5 * x * (1.0 + jnp.tanh(math.sqrt(2.0 / math.pi) * (x + 0.044715 * (x * x * x))))


def _mixout_kernel(x_ref, mod_ref, g_ref, y_ref, u_ref, bg_ref, cg_ref, v_ref, cgp_ref, vp_ref, cgn_ref, vn_ref,
                   kp_ref, kn_ref, d_ref, wglu_ref, bglu_ref, cw_ref, cb_ref, wout_ref, o_ref, *, d, halo):
    _, _, gate = _mod3(mod_ref, 1, d)
    sw = u_ref.shape[1]
    tm = x_ref.shape[0]
    z = _gelu_tanh(y_ref[...] + d_ref[...] * u_ref[...].astype(F32))
    s5 = z * jax.nn.sigmoid(_dot(z.astype(BF16), wglu_ref[...]) + bglu_ref[...])
    gv = cg_ref[...].astype(F32) * v_ref[...].astype(F32)
    gv_before = cgp_ref[halo - 1:halo, :].astype(F32) * vp_ref[halo - 1:halo, :].astype(F32)
    gv_after = cgn_ref[0:1, :].astype(F32) * vn_ref[0:1, :].astype(F32)
    row = lax.broadcasted_iota(jnp.int32, (tm, 1), 0)
    prev = jnp.where(row == 0, gv_before, pltpu.roll(gv, 1, 0)) * kp_ref[...]
    nxt = jnp.where(row == tm - 1, gv_after, pltpu.roll(gv, tm - 1, 0)) * kn_ref[...]
    conv = cw_ref[0:1, :] * prev + cw_ref[1:2, :] * gv + cw_ref[2:3, :] * nxt + cb_ref[...]
    conv = bg_ref[...].astype(F32) * conv
    y = _dot(s5.astype(BF16), wout_ref[:sw, :]) + _dot(conv.astype(BF16), wout_ref[sw:, :])
    o_ref[...] = x_ref[...] + gate * _rms(y, g_ref[3:4, :])


def _mixout(xt, mod, norm_g, ys, p, keep_prev, keep_next, s5_d, w_glu, b_glu, conv_w, conv_b, w_out, layer, i_even,
            lay):
    r, d = xt.shape
    sw = ys.shape[1]
    tm = min(lay.tm, 256)
    halo = 16
    hb = tm // halo
    last_halo = r // halo - 1
    col = lambda c: pl.BlockSpec((tm, sw), lambda i: (i, c))
    before = lambda c: pl.BlockSpec((halo, sw), lambda i: (jnp.maximum(i * hb - 1, 0), c))
    after = lambda c: pl.BlockSpec((halo, sw), lambda i: (jnp.minimum((i + 1) * hb, last_halo), c))
    vec = lambda n: pl.BlockSpec((None, 1, n), lambda i: (i_even, 0, 0))
    lay_m = lay.with_tm(tm)
    return pl.pallas_call(
        functools.partial(_mixout_kernel, d=d, halo=halo),
        grid=(r // tm,),
        in_specs=[pl.BlockSpec((tm, d), lambda i: (i, 0)),
                  pl.BlockSpec((None, None, 1, N_MOD * d), lambda i: (layer, lay_m.mod_row(i), 0, 0)),
                  pl.BlockSpec((None, 6, d), lambda i: (layer, 0, 0)),
                  pl.BlockSpec((tm, sw), lambda i: (i, 0)),
                  col(0), col(1), col(2), col(3), before(2), before(3), after(2), after(3),
                  pl.BlockSpec((tm, 1), lambda i: (i, 0)),
                  pl.BlockSpec((tm, 1), lambda i: (i, 0)),
                  vec(sw),
                  pl.BlockSpec((None, sw, sw), lambda i: (i_even, 0, 0)),
                  vec(sw),
                  pl.BlockSpec((None, 3, sw), lambda i: (i_even, 0, 0)),
                  vec(sw),
                  pl.BlockSpec((None, 2 * sw, d), lambda i: (i_even, 0, 0))],
        out_specs=pl.BlockSpec((tm, d), lambda i: (i, 0)),
        out_shape=jax.ShapeDtypeStruct((r, d), F32),
        compiler_params=_cparams(("parallel",)),
    )(xt, mod, norm_g, ys, p, p, p, p, p, p, p, p, keep_prev, keep_next,
      s5_d.reshape(s5_d.shape[0], 1, sw), w_glu, b_glu.reshape(b_glu.shape[0], 1, sw), conv_w,
      conv_b.reshape(conv_b.shape[0], 1, sw), w_out)


def _attn_kernel(lam_ref, qt_ref, kc_ref, vct_ref, k_ref, vt_ref, sub_ref, o_ref, q2_ref, s_ref, m_ref, al_ref, l_ref,
                 acc_ref, *, n_lat_q, tk, out_scale):
    qi = pl.program_id(2)
    hd, tq = qt_ref.shape
    qt = qt_ref[...]
    row = lax.broadcasted_iota(jnp.int32, (hd, tq), 0)
    zero = jnp.zeros_like(qt)
    q2_ref[:, :tq] = jnp.where(row < hd // 2, qt, zero)
    q2_ref[:, tq:] = jnp.where(row >= hd // 2, qt, zero)

    def scores(c):
        return _dot(k_ref[c * tk:(c + 1) * tk, :], q2_ref[...])

    s_ref[0] = scores(0)

    s = _dot(kc_ref[...], q2_ref[...])
    m0 = jnp.max(s, axis=0, keepdims=True)
    pr = jnp.exp2(s - m0)
    m_ref[...] = m0
    l_ref[...] = jnp.sum(pr, axis=0, keepdims=True)
    acc_ref[...] = _dot(vct_ref[...], pr.astype(BF16))

    @pl.when(qi < n_lat_q)
    def _():
        n_chunks = k_ref.shape[0] // tk

        def fold_max(buf):
            m_prev = m_ref[...]
            m_new = jnp.maximum(m_prev, jnp.max(s_ref[buf], axis=0, keepdims=True))
            al_ref[...] = jnp.exp2(m_prev - m_new)
            m_ref[...] = m_new

        fold_max(0)
        for c in range(n_chunks):
            cur, nxt, more = c % 2, (c + 1) % 2, c + 1 < n_chunks
            if more:
                s_ref[nxt] = scores(c + 1)
            alpha = al_ref[...]
            pr = jnp.exp2(s_ref[cur] - m_ref[...])
            l_ref[...] = alpha * l_ref[...] + jnp.sum(pr, axis=0, keepdims=True)
            pv = _dot(vt_ref[:, c * tk:(c + 1) * tk], pr.astype(BF16))
            if more:
                fold_max(nxt)
            acc_ref[...] = alpha * acc_ref[...] + pv

    o = acc_ref[...] * (1.0 / l_ref[...])
    o = o[:, :tq] - lam_ref[0] * o[:, tq:]
    o = o * lax.rsqrt(jnp.mean(o * o, axis=0, keepdims=True) + EPS)
    o = o * jnp.tile(sub_ref[...], (1, tq // LANES)) * out_scale
    o_ref[...] = o.T.astype(o_ref.dtype)


def _attn(qkv, lam, subln, lam_init, lay):
    r = qkv.shape[0]
    hd = subln.shape[-1]
    heads = qkv.shape[1] // (3 * hd)
    d = heads * hd
    tq = lay.ctx
    tk = _pick(lay.seq, (512, 256, 128))
    n_lat_q = lay.seq // tq
    lat_blocks = lay.batch * n_lat_q
    qt = qkv[:, :d].T
    vt = qkv[:, 2 * d:].T

    def q_block(b, qi):
        return jnp.where(qi < n_lat_q, b * n_lat_q + qi, lat_blocks + b)

    return pl.pallas_call(
        functools.partial(_attn_kernel, n_lat_q=n_lat_q, tk=tk, out_scale=1.0 - lam_init),
        grid=(lay.batch, heads, n_lat_q + 1),
        in_specs=[pl.BlockSpec(memory_space=pltpu.SMEM),
                  pl.BlockSpec((hd, tq), lambda b, h, qi: (h, q_block(b, qi))),
                  pl.BlockSpec((lay.ctx, hd), lambda b, h, qi: (lat_blocks + b, heads + h)),
                  pl.BlockSpec((hd, lay.ctx), lambda b, h, qi: (h, lat_blocks + b)),
                  pl.BlockSpec((lay.seq, hd), lambda b, h, qi: (b, heads + h)),
                  pl.BlockSpec((hd, lay.seq), lambda b, h, qi: (h, b)),
                  pl.BlockSpec((hd, LANES), lambda b, h, qi: (0, 0))],
        out_specs=pl.BlockSpec((tq, hd), lambda b, h, qi: (q_block(b, qi), h)),
        out_shape=jax.ShapeDtypeStruct((r, d), BF16),
        scratch_shapes=[pltpu.VMEM((hd, 2 * tq), BF16), pltpu.VMEM((2, tk, 2 * tq), F32)]
        + [pltpu.VMEM((1, 2 * tq), F32)] * 3 + [pltpu.VMEM((hd, 2 * tq), F32)],
        compiler_params=_cparams(("parallel", "parallel", "arbitrary")),
    )(lam, qt, qkv, vt, qkv, vt, jnp.broadcast_to(subln.reshape(hd, 1), (hd, LANES)))


def _attnout_kernel(x_ref, mod_ref, g_ref, a_ref, w_ref, o_ref, *, d):
    _, _, gate = _mod3(mod_ref, 1, d)
    y = _dot(a_ref[...], w_ref[...])
    o_ref[...] = x_ref[...] + gate * _rms(y, g_ref[3:4, :])


def _attnout(xt, mod, norm_g, a, w_o, layer, i_odd, lay):
    r, d = xt.shape
    tm = min(lay.tm, 256)
    lay_m = lay.with_tm(tm)
    return pl.pallas_call(
        functools.partial(_attnout_kernel, d=d),
        grid=(r // tm,),
        in_specs=[pl.BlockSpec((tm, d), lambda i: (i, 0)),
                  pl.BlockSpec((None, None, 1, N_MOD * d), lambda i: (layer, lay_m.mod_row(i), 0, 0)),
                  pl.BlockSpec((None, 6, d), lambda i: (layer, 0, 0)),
                  pl.BlockSpec((tm, a.shape[1]), lambda i: (i, 0)),
                  pl.BlockSpec((None, a.shape[1], d), lambda i: (i_odd, 0, 0))],
        out_specs=pl.BlockSpec((tm, d), lambda i: (i, 0)),
        out_shape=jax.ShapeDtypeStruct((r, d), F32),
        compiler_params=_cparams(("parallel",)),
    )(xt, mod, norm_g, a, w_o)


class _Layout:
    def __init__(self, batch, seq, ctx, tm, q_scale):
        self.batch, self.seq, self.ctx, self.tm, self.q_scale = batch, seq, ctx, tm, q_scale
        self.tiles_per_seq = seq // tm
        self.n_lat_tiles = batch * self.tiles_per_seq

    def with_tm(self, tm):
        return _Layout(self.batch, self.seq, self.ctx, tm, self.q_scale)

    def mod_row(self, i):
        return jnp.minimum(i // self.tiles_per_seq, self.batch)


def kernel(x, c, ctx, c_ctx, w_mod, b_mod, norm_g, ffn_wg, ffn_wu, ffn_wd, mix_w_in, mix_w_out, s5_lam_re, s5_lam_im, s5_log_step, s5_b_re, s5_b_im, s5_c_re, s5_c_im, s5_d, s5_w_glu, s5_b_glu, conv_w, conv_b, attn_w_qkv, attn_w_o, attn_lambda, attn_subln):
    batch, seq, d = x.shape
    n_ctx = ctx.shape[1]
    depth = w_mod.shape[0]
    hd = attn_subln.shape[-1]
    dh = attn_lambda.shape[-1]
    g, p, h = s5_b_re.shape[2:]
    sw = g * h
    t = S5_CHUNK
    assert batch + 1 <= SUBLANES and conv_w.shape[-1] == sw and mix_w_in.shape[-1] == 4 * sw
    assert g % S5_BLOCK == 0 and (S5_BLOCK * p) % LANES == 0 and (S5_BLOCK * h * t) % LANES == 0
    assert seq % (t * SUBLANES) == 0 and n_ctx % (t * SUBLANES) == 0 and seq % n_ctx == 0 and hd == 2 * dh
    tm = _pick(math.gcd(seq, batch * n_ctx), (512, 256, 128, 64, 32, 16))
    lay = _Layout(batch, seq, n_ctx, tm, dh ** -0.5 * math.log2(math.e))
    r = batch * (seq + n_ctx)

    xt = jnp.concatenate([x.reshape(batch * seq, d), ctx.reshape(batch * n_ctx, d)], axis=0)
    cc = jnp.zeros((SUBLANES, d), F32).at[:batch].set(c).at[batch].set(c_ctx)
    mod = _ada(cc, w_mod, b_mod).reshape(depth, SUBLANES, 1, N_MOD * d)

    wg, wu, wd = ffn_wg.astype(BF16), ffn_wu.astype(BF16), ffn_wd.astype(BF16)
    w_in, w_out, w_glu = mix_w_in.astype(BF16), mix_w_out.astype(BF16), s5_w_glu.astype(BF16)
    w_qkv, w_o = attn_w_qkv.astype(BF16), attn_w_o.astype(BF16)

    cos, sin = _rope_tables(seq)
    pos = jnp.concatenate([jnp.tile(jnp.arange(seq), batch), jnp.tile(jnp.arange(n_ctx), batch)])
    last = jnp.concatenate([jnp.full((batch * seq,), seq - 1), jnp.full((batch * n_ctx,), n_ctx - 1)])
    keep_prev = (pos != 0).astype(F32)[:, None]
    keep_next = (pos != last).astype(F32)[:, None]
    nc = r // t
    mats = jax.vmap(functools.partial(_s5_matrices, t=t))(s5_lam_re, s5_lam_im, s5_log_step, s5_b_re, s5_b_im,
                                                          s5_c_re, s5_c_im)

    for layer in range(depth):
        i = layer // 2
        xt = _ffn(xt, mod, norm_g, wg, wu, wd, layer, 0, 0, lay)
        if layer % 2 == 0:
            pj = _proj(xt, mod, norm_g, w_in, i, layer, lay)
            nb, wb = g // S5_BLOCK, S5_BLOCK * h
            uf = pj[:, :sw].reshape(nc, t, nb, wb).transpose(2, 0, 1, 3).reshape(nb, nc, t * wb)
            ys = _s5(uf, mats, i, lay)
            ys = ys.reshape(nb, nc, t, wb).transpose(1, 2, 0, 3).reshape(r, sw)
            xt = _mixout(xt, mod, norm_g, ys, pj, keep_prev, keep_next, s5_d, w_glu, s5_b_glu, conv_w, conv_b, w_out,
                         layer, i, lay)
        else:
            lam_init = 0.8 - 0.6 * math.exp(-0.3 * layer)
            lv = attn_lambda[i].astype(F32)
            lam = (jnp.exp(jnp.sum(lv[0] * lv[1])) - jnp.exp(jnp.sum(lv[2] * lv[3])) + lam_init).reshape(1)
            qkv = _proj(xt, mod, norm_g, w_qkv, i, layer, lay, rope=(cos, sin, 2 * d, d))
            a = _attn(qkv, lam, attn_subln[i], lam_init, lay)
            xt = _attnout(xt, mod, norm_g, a, w_o, layer, i, lay)
        xt = _ffn(xt, mod, norm_g, wg, wu, wd, layer, 2, 1, lay, rows=batch * seq if layer == depth - 1 else None)
    return xt.reshape(batch, seq, d)
```

```python
import functools
import math

import jax
import jax.numpy as jnp
from jax import lax
from jax.experimental import pallas as pl
from jax.experimental.pallas import tpu as pltpu

F32 = jnp.float32
BF16 = jnp.bfloat16
EPS = 1e-6
N_MOD = 9
GRID_W = 64
ROPE_BASE = 10000.0
S5_CHUNK = 16
S5_BLOCK = 4
SUBLANES = 8
LANES = 128
ROW_CHUNK = 16
ROW_UNROLL = 8
VMEM_LIMIT = 56 * 1024 * 1024


def _cparams(sem):
    return pltpu.CompilerParams(dimension_semantics=sem, vmem_limit_bytes=VMEM_LIMIT)


def _dot(a, b):
    return jnp.dot(a, b, preferred_element_type=F32)


def _rms(x, g):
    return x * lax.rsqrt(jnp.mean(x * x, axis=-1, keepdims=True) + EPS) * g


def _for_row_chunks(n_rows, fn):
    def body(i, carry):
        fn(pl.ds(pl.multiple_of(i * ROW_CHUNK, ROW_CHUNK), ROW_CHUNK))
        return carry
    lax.fori_loop(0, n_rows // ROW_CHUNK, body, 0, unroll=ROW_UNROLL)


def _norm_modulate(x_ref, h_ref, g, scale, shift):
    gs = g * (1.0 + scale)

    def chunk(rows):
        x = x_ref[rows, :]
        h_ref[rows, :] = (_rms(x, gs) + shift).astype(h_ref.dtype)
    _for_row_chunks(x_ref.shape[0], chunk)


def _gated_residual(x_ref, y_ref, o_ref, gg):
    def chunk(rows):
        o_ref[rows, :] = x_ref[rows, :] + _rms(y_ref[rows, :], gg)
    _for_row_chunks(x_ref.shape[0], chunk)


def _pick(n, candidates):
    for c in candidates:
        if n % c == 0:
            return c
    return n


def _ada_kernel(c_ref, w_ref, b_ref, o_ref):
    s = c_ref[...]
    s = s * jax.nn.sigmoid(s)
    o_ref[...] = _dot(s.astype(BF16), w_ref[...].astype(BF16)) + b_ref[...]


def _ada(cc, w_mod, b_mod):
    depth, d, n = w_mod.shape
    tn = _pick(n, (1024, 512, 256, 128))
    return pl.pallas_call(
        _ada_kernel,
        grid=(depth, n // tn),
        in_specs=[pl.BlockSpec((SUBLANES, d), lambda l, j: (0, 0)),
                  pl.BlockSpec((None, d, tn), lambda l, j: (l, 0, j)),
                  pl.BlockSpec((None, 1, tn), lambda l, j: (l, 0, j))],
        out_specs=pl.BlockSpec((None, SUBLANES, tn), lambda l, j: (l, 0, j)),
        out_shape=jax.ShapeDtypeStruct((depth, SUBLANES, n), F32),
        compiler_params=_cparams(("arbitrary", "arbitrary")),
    )(cc, w_mod, b_mod.reshape(depth, 1, n))


def _mod3(mod_ref, sub, d):
    return tuple(mod_ref[:, (3 * sub + k) * d:(3 * sub + k + 1) * d] for k in range(3))


def _ffn_kernel(x_ref, mod_ref, g_ref, wg_ref, wu_ref, wd_ref, o_ref, h_ref, acc_ref, *, sub, d):
    j = pl.program_id(1)
    shift, scale, gate = _mod3(mod_ref, sub, d)

    @pl.when(j == 0)
    def _():
        _norm_modulate(x_ref, h_ref, g_ref[2 * sub:2 * sub + 1, :], scale, shift)
        acc_ref[...] = jnp.zeros_like(acc_ref)

    h = h_ref[...]
    a = _dot(h, wg_ref[...])
    u = _dot(h, wu_ref[...])
    act = (a * jax.nn.sigmoid(a) * u).astype(BF16)
    acc_ref[...] += _dot(act, wd_ref[...])

    @pl.when(j == pl.num_programs(1) - 1)
    def _():
        _gated_residual(x_ref, acc_ref, o_ref, 0.5 * gate * g_ref[2 * sub + 1:2 * sub + 2, :])


def _ffn(xt, mod, norm_g, wg, wu, wd, layer, sub, widx, lay, rows=None):
    d = xt.shape[1]
    r = xt.shape[0] if rows is None else rows
    f = wg.shape[-1]
    tm, tf = lay.tm, _pick(f, (512, 256, 128))
    return pl.pallas_call(
        functools.partial(_ffn_kernel, sub=sub, d=d),
        grid=(r // tm, f // tf),
        in_specs=[pl.BlockSpec((tm, d), lambda i, j: (i, 0)),
                  pl.BlockSpec((None, None, 1, N_MOD * d), lambda i, j: (layer, lay.mod_row(i), 0, 0)),
                  pl.BlockSpec((None, 6, d), lambda i, j: (layer, 0, 0)),
                  pl.BlockSpec((None, None, d, tf), lambda i, j: (layer, widx, 0, j)),
                  pl.BlockSpec((None, None, d, tf), lambda i, j: (layer, widx, 0, j)),
                  pl.BlockSpec((None, None, tf, d), lambda i, j: (layer, widx, j, 0))],
        out_specs=pl.BlockSpec((tm, d), lambda i, j: (i, 0)),
        out_shape=jax.ShapeDtypeStruct((r, d), F32),
        scratch_shapes=[pltpu.VMEM((tm, d), BF16), pltpu.VMEM((tm, d), F32)],
        compiler_params=_cparams(("parallel", "arbitrary")),
    )(xt, mod, norm_g, wg, wu, wd)


def _proj_kernel(x_ref, mod_ref, g_ref, w_ref, cos_ref, sin_ref, o_ref, h_ref, *, d, n_lat_tiles, n_rot_tiles,
                 n_q_tiles, q_scale):
    i, j = pl.program_id(0), pl.program_id(1)
    shift, scale, _ = _mod3(mod_ref, 1, d)

    @pl.when(j == 0)
    def _():
        _norm_modulate(x_ref, h_ref, g_ref[2:3, :], scale, shift)

    y = _dot(h_ref[...], w_ref[...])
    if n_q_tiles:
        y = y * jnp.where(j < n_q_tiles, q_scale, 1.0)
    rotate = jnp.logical_and(i < n_lat_tiles, j < n_rot_tiles)

    @pl.when(rotate)
    def _():
        tn = y.shape[1]
        lane = lax.broadcasted_iota(jnp.int32, (1, LANES), 1)
        first_half = (lane % 64) < 32
        for c in range(tn // LANES):
            yc = y[:, c * LANES:(c + 1) * LANES]
            partner = jnp.where(first_half, pltpu.roll(yc, LANES - 32, 1), pltpu.roll(yc, 32, 1))
            o_ref[:, c * LANES:(c + 1) * LANES] = (yc * cos_ref[...] + partner * sin_ref[...]).astype(o_ref.dtype)

    @pl.when(jnp.logical_not(rotate))
    def _():
        o_ref[...] = y.astype(o_ref.dtype)


def _proj(xt, mod, norm_g, w, widx, layer, lay, rope=None):
    r, d = xt.shape
    n = w.shape[-1]
    tm = lay.tm
    if rope is None:
        tn = _pick(n, (1024, 512, 256, 128))
        cos = sin = jnp.zeros((tm, LANES), F32)
        n_rot = n_q = 0
        tab_map = lambda i, j: (0, 0)
    else:
        cos, sin, n_rot, n_q = rope
        tn = _pick(math.gcd(n_q, n), (1024, 512, 256, 128))
        tab_map = lambda i, j: (i % lay.tiles_per_seq, 0)
    return pl.pallas_call(
        functools.partial(_proj_kernel, d=d, n_lat_tiles=lay.n_lat_tiles, n_rot_tiles=n_rot // tn,
                          n_q_tiles=n_q // tn, q_scale=lay.q_scale),
        grid=(r // tm, n // tn),
        in_specs=[pl.BlockSpec((tm, d), lambda i, j: (i, 0)),
                  pl.BlockSpec((None, None, 1, N_MOD * d), lambda i, j: (layer, lay.mod_row(i), 0, 0)),
                  pl.BlockSpec((None, 6, d), lambda i, j: (layer, 0, 0)),
                  pl.BlockSpec((None, d, tn), lambda i, j: (widx, 0, j)),
                  pl.BlockSpec((tm, LANES), tab_map),
                  pl.BlockSpec((tm, LANES), tab_map)],
        out_specs=pl.BlockSpec((tm, tn), lambda i, j: (i, j)),
        out_shape=jax.ShapeDtypeStruct((r, n), BF16),
        scratch_shapes=[pltpu.VMEM((tm, d), BF16)],
        compiler_params=_cparams(("parallel", "arbitrary")),
    )(xt, mod, norm_g, w, cos, sin)


def _rope_tables(n):
    rows = n // GRID_W
    row = jnp.repeat(jnp.arange(rows), GRID_W).astype(F32)
    col = jnp.tile(jnp.arange(GRID_W), rows).astype(F32)
    pairs = 16
    freqs = jnp.power(ROPE_BASE, -jnp.arange(pairs, dtype=F32) / pairs)
    ang = jnp.concatenate([row[:, None] * freqs, col[:, None] * freqs], axis=-1)
    cos, sin = jnp.cos(ang), jnp.sin(ang)
    return jnp.tile(cos, (1, 4)), jnp.concatenate([-sin, sin, -sin, sin], axis=-1)


def _s5_matrices(lam_re, lam_im, log_step, b_re, b_im, c_re, c_im, t):
    hp = lax.Precision.HIGHEST
    lam_re, lam_im = lam_re.astype(F32), lam_im.astype(F32)
    dt = jnp.exp(log_step.astype(F32))[..., None]
    lr, li = lam_re * dt, lam_im * dt
    mag = jnp.exp(lr)
    a_re, a_im = mag * jnp.cos(li), mag * jnp.sin(li)
    inv = 1.0 / (lam_re * lam_re + lam_im * lam_im)
    co_re = ((a_re - 1.0) * lam_re + a_im * lam_im) * inv
    co_im = (a_im * lam_re - (a_re - 1.0) * lam_im) * inv
    b_re, b_im = b_re.astype(F32), b_im.astype(F32)
    bb_re = co_re[..., None] * b_re - co_im[..., None] * b_im
    bb_im = co_re[..., None] * b_im + co_im[..., None] * b_re
    c_re, c_im = c_re.astype(F32), c_im.astype(F32)
    tau = jnp.arange(t + 1, dtype=F32)[:, None, None, None]
    pmag = jnp.exp(tau * lr[None])
    pw_re, pw_im = pmag * jnp.cos(tau * li[None]), pmag * jnp.sin(tau * li[None])
    cp_re = c_re[None] * pw_re[:, :, :, None, :] - c_im[None] * pw_im[:, :, :, None, :]
    cp_im = c_re[None] * pw_im[:, :, :, None, :] + c_im[None] * pw_re[:, :, :, None, :]
    kern = (jnp.einsum('tdghp,dgpk->tdghk', cp_re, bb_re, precision=hp)
            - jnp.einsum('tdghp,dgpk->tdghk', cp_im, bb_im, precision=hp))
    g, p, h = b_re.shape[1:]
    s_idx = jnp.arange(t)[:, None]
    t_idx = jnp.arange(t)[None, :]

    gb = S5_BLOCK
    nb = g // gb
    th, wide = t * h, t * gb * h
    col = jnp.arange(wide)
    row = jnp.arange(th)
    grp = jnp.arange(gb)[:, None, None]
    sel_th = ((row[None, :, None] // h == col[None, None, :] // (gb * h)) & (row[None, :, None] % h == col[None, None, :] % h)
              & ((col[None, None, :] // h) % gb == grp)).astype(BF16)
    colp = jnp.arange(gb * p)
    sel_p = ((jnp.arange(p)[None, :, None] == colp[None, None, :] % p) & (colp[None, None, :] // p == grp)).astype(BF16)

    def widen(m, sel):
        return jnp.einsum('qerc,ecy->qery', m.astype(BF16).astype(F32), sel.astype(F32)).astype(BF16)

    def interleave(m):
        return m.reshape(nb, gb, t, h, m.shape[-1]).transpose(0, 2, 1, 3, 4).reshape(nb, wide, m.shape[-1])

    def toeplitz(k_dir, lag):
        m = jnp.where((lag >= 0)[:, :, None, None, None], k_dir[jnp.clip(lag, 0, t)], 0.0)
        return m.transpose(2, 0, 4, 1, 3).reshape(nb, gb, th, th)

    mk = interleave(widen(toeplitz(kern[:, 0], t_idx - s_idx) + toeplitz(kern[:, 1], s_idx - t_idx), sel_th))

    def drive(d, exps):
        wr, wi = pw_re[exps, d], pw_im[exps, d]
        re = wr[..., None] * bb_re[d][None] - wi[..., None] * bb_im[d][None]
        im = wr[..., None] * bb_im[d][None] + wi[..., None] * bb_re[d][None]
        f = lambda m: interleave(widen(m.reshape(t, nb, gb, p, h).transpose(1, 2, 0, 4, 3).reshape(nb, gb, th, p), sel_p))
        return jnp.concatenate([f(re), f(im)], axis=-1)

    def read(d, exps):
        f = lambda m: widen(m.reshape(t, nb, gb, h, p).transpose(1, 2, 4, 0, 3).reshape(nb, gb, p, th),
                            sel_th).reshape(nb, gb * p, wide)
        return jnp.concatenate([f(cp_re[exps, d]), f(-cp_im[exps, d])], axis=1)

    ar = jnp.arange(t)
    ms_f, ms_r = drive(0, t - 1 - ar), drive(1, ar)
    mo_f, mo_r = read(0, ar + 1), read(1, t - ar)
    at = jnp.stack([pw_re[t, 0], pw_im[t, 0], pw_re[t, 1], pw_im[t, 1]], axis=1)
    at = at.reshape(nb, gb, 4, p).transpose(0, 2, 1, 3).reshape(nb, 4, gb * p)
    return mk, ms_f, ms_r, mo_f, mo_r, at


def _s5_kernel(uf_ref, mk_ref, msf_ref, msr_ref, mof_ref, mor_ref, a_ref, y_ref, sf_ref, sr_ref, hf_ref, hr_ref, *,
               batch, lat_tiles, ctx_tiles):
    w = a_ref.shape[-1]
    uf = uf_ref[...]
    sf_ref[...] = _dot(uf, msf_ref[...])
    sr_ref[...] = _dot(uf, msr_ref[...])
    a = a_ref[...]
    af = (a[0:1], a[1:2])
    ab = (a[2:3], a[3:4])
    row_id = lax.broadcasted_iota(jnp.int32, (SUBLANES, 2 * w), 0)

    def tile_step(s_ref, hp_ref, tile, state, coef, rows):
        off = pl.multiple_of(tile * SUBLANES, SUBLANES)
        s = s_ref[pl.ds(off, SUBLANES), :]
        hre, him = state
        are, aim = coef
        prev = jnp.zeros((SUBLANES, 2 * w), F32)
        for i in rows:
            row = jnp.concatenate([hre, him], axis=1)
            prev = jnp.where(row_id == i, jnp.broadcast_to(row, prev.shape), prev)
            sre, sim = s[i:i + 1, :w], s[i:i + 1, w:]
            hre, him = are * hre - aim * him + sre, are * him + aim * hre + sim
        hp_ref[pl.ds(off, SUBLANES), :] = prev
        return hre, him

    zero = jnp.zeros((1, w), F32)
    state = tuple((zero, zero) for _ in range(2 * batch))
    asc, desc = tuple(range(SUBLANES)), tuple(reversed(range(SUBLANES)))

    def phase(first_tile, n_tiles, state):
        def body(k, st):
            out = []
            for b in range(batch):
                out.append(tile_step(sf_ref, hf_ref, first_tile(b) + k, st[2 * b], af, asc))
                out.append(tile_step(sr_ref, hr_ref, first_tile(b) + n_tiles - 1 - k, st[2 * b + 1], ab, desc))
            return tuple(out)
        return lax.fori_loop(0, n_tiles, body, state)

    state = phase(lambda b: batch * lat_tiles + b * ctx_tiles, ctx_tiles, state)
    phase(lambda b: b * lat_tiles, lat_tiles, state)

    y_ref[...] = (_dot(uf, mk_ref[...]) + _dot(hf_ref[...].astype(BF16), mof_ref[...])
                  + _dot(hr_ref[...].astype(BF16), mor_ref[...]))


def _s5(uf, mats, i_even, lay):
    mk, ms_f, ms_r, mo_f, mo_r, at = mats
    nb, nc, wu = uf.shape
    ws = ms_f.shape[-1]
    t = S5_CHUNK
    full = lambda *shape: pl.BlockSpec((None,) + shape, lambda q: (q,) + (0,) * len(shape))
    mat = lambda *shape: pl.BlockSpec((None, None) + shape, lambda q: (i_even, q) + (0,) * len(shape))
    return pl.pallas_call(
        functools.partial(_s5_kernel, batch=lay.batch, lat_tiles=lay.seq // (t * SUBLANES),
                          ctx_tiles=lay.ctx // (t * SUBLANES)),
        grid=(nb,),
        in_specs=[full(nc, wu), mat(wu, wu), mat(wu, ws), mat(wu, ws), mat(ws, wu), mat(ws, wu),
                  mat(4, ws // 2)],
        out_specs=full(nc, wu),
        out_shape=jax.ShapeDtypeStruct((nb, nc, wu), F32),
        scratch_shapes=[pltpu.VMEM((nc, ws), F32)] * 4,
        compiler_params=_cparams(("parallel",)),
    )(uf, mk, ms_f, ms_r, mo_f, mo_r, at)


def _gelu_tanh(x):
    return 0.5 * x * (1.0 + jnp.tanh(math.sqrt(2.0 / math.pi) * (x + 0.044715 * (x * x * x))))


def _mixout_kernel(x_ref, mod_ref, g_ref, y_ref, u_ref, bg_ref, cg_ref, v_ref, cgp_ref, vp_ref, cgn_ref, vn_ref,
                   kp_ref, kn_ref, d_ref, wglu_ref, bglu_ref, cw_ref, cb_ref, wout_ref, o_ref, *, d, halo):
    _, _, gate = _mod3(mod_ref, 1, d)
    sw = u_ref.shape[1]
    tm = x_ref.shape[0]
    z = _gelu_tanh(y_ref[...] + d_ref[...] * u_ref[...].astype(F32))
    s5 = z * jax.nn.sigmoid(_dot(z.astype(BF16), wglu_ref[...]) + bglu_ref[...])
    gv = cg_ref[...].astype(F32) * v_ref[...].astype(F32)
    gv_before = cgp_ref[halo - 1:halo, :].astype(F32) * vp_ref[halo - 1:halo, :].astype(F32)
    gv_after = cgn_ref[0:1, :].astype(F32) * vn_ref[0:1, :].astype(F32)
    row = lax.broadcasted_iota(jnp.int32, (tm, 1), 0)
    prev = jnp.where(row == 0, gv_before, pltpu.roll(gv, 1, 0)) * kp_ref[...]
    nxt = jnp.where(row == tm - 1, gv_after, pltpu.roll(gv, tm - 1, 0)) * kn_ref[...]
    conv = cw_ref[0:1, :] * prev + cw_ref[1:2, :] * gv + cw_ref[2:3, :] * nxt + cb_ref[...]
    conv = bg_ref[...].astype(F32) * conv
    y = _dot(s5.astype(BF16), wout_ref[:sw, :]) + _dot(conv.astype(BF16), wout_ref[sw:, :])
    o_ref[...] = x_ref[...] + gate * _rms(y, g_ref[3:4, :])


def _mixout(xt, mod, norm_g, ys, p, keep_prev, keep_next, s5_d, w_glu, b_glu, conv_w, conv_b, w_out, layer, i_even,
            lay):
    r, d = xt.shape
    sw = ys.shape[1]
    tm = min(lay.tm, 256)
    halo = 16
    hb = tm // halo
    last_halo = r // halo - 1
    col = lambda c: pl.BlockSpec((tm, sw), lambda i: (i, c))
    before = lambda c: pl.BlockSpec((halo, sw), lambda i: (jnp.maximum(i * hb - 1, 0), c))
    after = lambda c: pl.BlockSpec((halo, sw), lambda i: (jnp.minimum((i + 1) * hb, last_halo), c))
    vec = lambda n: pl.BlockSpec((None, 1, n), lambda i: (i_even, 0, 0))
    lay_m = lay.with_tm(tm)
    return pl.pallas_call(
        functools.partial(_mixout_kernel, d=d, halo=halo),
        grid=(r // tm,),
        in_specs=[pl.BlockSpec((tm, d), lambda i: (i, 0)),
                  pl.BlockSpec((None, None, 1, N_MOD * d), lambda i: (layer, lay_m.mod_row(i), 0, 0)),
                  pl.BlockSpec((None, 6, d), lambda i: (layer, 0, 0)),
                  pl.BlockSpec((tm, sw), lambda i: (i, 0)),
                  col(0), col(1), col(2), col(3), before(2), before(3), after(2), after(3),
                  pl.BlockSpec((tm, 1), lambda i: (i, 0)),
                  pl.BlockSpec((tm, 1), lambda i: (i, 0)),
                  vec(sw),
                  pl.BlockSpec((None, sw, sw), lambda i: (i_even, 0, 0)),
                  vec(sw),
                  pl.BlockSpec((None, 3, sw), lambda i: (i_even, 0, 0)),
                  vec(sw),
                  pl.BlockSpec((None, 2 * sw, d), lambda i: (i_even, 0, 0))],
        out_specs=pl.BlockSpec((tm, d), lambda i: (i, 0)),
        out_shape=jax.ShapeDtypeStruct((r, d), F32),
        compiler_params=_cparams(("parallel",)),
    )(xt, mod, norm_g, ys, p, p, p, p, p, p, p, p, keep_prev, keep_next,
      s5_d.reshape(s5_d.shape[0], 1, sw), w_glu, b_glu.reshape(b_glu.shape[0], 1, sw), conv_w,
      conv_b.reshape(conv_b.shape[0], 1, sw), w_out)


def _attn_kernel(lam_ref, qt_ref, kc_ref, vct_ref, k_ref, vt_ref, sub_ref, o_ref, q2_ref, s_ref, m_ref, al_ref, l_ref,
                 acc_ref, *, n_lat_q, tk, out_scale):
    qi = pl.program_id(2)
    hd, tq = qt_ref.shape
    qt = qt_ref[...]
    row = lax.broadcasted_iota(jnp.int32, (hd, tq), 0)
    zero = jnp.zeros_like(qt)
    q2_ref[:, :tq] = jnp.where(row < hd // 2, qt, zero)
    q2_ref[:, tq:] = jnp.where(row >= hd // 2, qt, zero)

    def scores(c):
        return _dot(k_ref[c * tk:(c + 1) * tk, :], q2_ref[...])

    s_ref[0] = scores(0)

    s = _dot(kc_ref[...], q2_ref[...])
    m0 = jnp.max(s, axis=0, keepdims=True)
    pr = jnp.exp2(s - m0)
    m_ref[...] = m0
    l_ref[...] = jnp.sum(pr, axis=0, keepdims=True)
    acc_ref[...] = _dot(vct_ref[...], pr.astype(BF16))

    @pl.when(qi < n_lat_q)
    def _():
        n_chunks = k_ref.shape[0] // tk

        def fold_max(buf):
            m_prev = m_ref[...]
            m_new = jnp.maximum(m_prev, jnp.max(s_ref[buf], axis=0, keepdims=True))
            al_ref[...] = jnp.exp2(m_prev - m_new)
            m_ref[...] = m_new

        fold_max(0)
        for c in range(n_chunks):
            cur, nxt, more = c % 2, (c + 1) % 2, c + 1 < n_chunks
            if more:
                s_ref[nxt] = scores(c + 1)
            alpha = al_ref[...]
            pr = jnp.exp2(s_ref[cur] - m_ref[...])
            l_ref[...] = alpha * l_ref[...] + jnp.sum(pr, axis=0, keepdims=True)
            pv = _dot(vt_ref[:, c * tk:(c + 1) * tk], pr.astype(BF16))
            if more:
                fold_max(nxt)
            acc_ref[...] = alpha * acc_ref[...] + pv

    o = acc_ref[...] * (1.0 / l_ref[...])
    o = o[:, :tq] - lam_ref[0] * o[:, tq:]
    o = o * lax.rsqrt(jnp.mean(o * o, axis=0, keepdims=True) + EPS)
    o = o * jnp.tile(sub_ref[...], (1, tq // LANES)) * out_scale
    o_ref[...] = o.T.astype(o_ref.dtype)


def _attn(qkv, lam, subln, lam_init, lay):
    r = qkv.shape[0]
    hd = subln.shape[-1]
    heads = qkv.shape[1] // (3 * hd)
    d = heads * hd
    tq = lay.ctx
    tk = _pick(lay.seq, (512, 256, 128))
    n_lat_q = lay.seq // tq
    lat_blocks = lay.batch * n_lat_q
    qt = qkv[:, :d].T
    vt = qkv[:, 2 * d:].T

    def q_block(b, qi):
        return jnp.where(qi < n_lat_q, b * n_lat_q + qi, lat_blocks + b)

    return pl.pallas_call(
        functools.partial(_attn_kernel, n_lat_q=n_lat_q, tk=tk, out_scale=1.0 - lam_init),
        grid=(lay.batch, heads, n_lat_q + 1),
        in_specs=[pl.BlockSpec(memory_space=pltpu.SMEM),
                  pl.BlockSpec((hd, tq), lambda b, h, qi: (h, q_block(b, qi))),
                  pl.BlockSpec((lay.ctx, hd), lambda b, h, qi: (lat_blocks + b, heads + h)),
                  pl.BlockSpec((hd, lay.ctx), lambda b, h, qi: (h, lat_blocks + b)),
                  pl.BlockSpec((lay.seq, hd), lambda b, h, qi: (b, heads + h)),
                  pl.BlockSpec((hd, lay.seq), lambda b, h, qi: (h, b)),
                  pl.BlockSpec((hd, LANES), lambda b, h, qi: (0, 0))],
        out_specs=pl.BlockSpec((tq, hd), lambda b, h, qi: (q_block(b, qi), h)),
        out_shape=jax.ShapeDtypeStruct((r, d), BF16),
        scratch_shapes=[pltpu.VMEM((hd, 2 * tq), BF16), pltpu.VMEM((2, tk, 2 * tq), F32)]
        + [pltpu.VMEM((1, 2 * tq), F32)] * 3 + [pltpu.VMEM((hd, 2 * tq), F32)],
        compiler_params=_cparams(("parallel", "parallel", "arbitrary")),
    )(lam, qt, qkv, vt, qkv, vt, jnp.broadcast_to(subln.reshape(hd, 1), (hd, LANES)))


def _attnout_kernel(x_ref, mod_ref, g_ref, a_ref, w_ref, o_ref, *, d):
    _, _, gate = _mod3(mod_ref, 1, d)
    y = _dot(a_ref[...], w_ref[...])
    o_ref[...] = x_ref[...] + gate * _rms(y, g_ref[3:4, :])


def _attnout(xt, mod, norm_g, a, w_o, layer, i_odd, lay):
    r, d = xt.shape
    tm = min(lay.tm, 256)
    lay_m = lay.with_tm(tm)
    return pl.pallas_call(
        functools.partial(_attnout_kernel, d=d),
        grid=(r // tm,),
        in_specs=[pl.BlockSpec((tm, d), lambda i: (i, 0)),
                  pl.BlockSpec((None, None, 1, N_MOD * d), lambda i: (layer, lay_m.mod_row(i), 0, 0)),
                  pl.BlockSpec((None, 6, d), lambda i: (layer, 0, 0)),
                  pl.BlockSpec((tm, a.shape[1]), lambda i: (i, 0)),
                  pl.BlockSpec((None, a.shape[1], d), lambda i: (i_odd, 0, 0))],
        out_specs=pl.BlockSpec((tm, d), lambda i: (i, 0)),
        out_shape=jax.ShapeDtypeStruct((r, d), F32),
        compiler_params=_cparams(("parallel",)),
    )(xt, mod, norm_g, a, w_o)


class _Layout:
    def __init__(self, batch, seq, ctx, tm, q_scale):
        self.batch, self.seq, self.ctx, self.tm, self.q_scale = batch, seq, ctx, tm, q_scale
        self.tiles_per_seq = seq // tm
        self.n_lat_tiles = batch * self.tiles_per_seq

    def with_tm(self, tm):
        return _Layout(self.batch, self.seq, self.ctx, tm, self.q_scale)

    def mod_row(self, i):
        return jnp.minimum(i // self.tiles_per_seq, self.batch)


def kernel(x, c, ctx, c_ctx, w_mod, b_mod, norm_g, ffn_wg, ffn_wu, ffn_wd, mix_w_in, mix_w_out, s5_lam_re, s5_lam_im, s5_log_step, s5_b_re, s5_b_im, s5_c_re, s5_c_im, s5_d, s5_w_glu, s5_b_glu, conv_w, conv_b, attn_w_qkv, attn_w_o, attn_lambda, attn_subln):
    batch, seq, d = x.shape
    n_ctx = ctx.shape[1]
    depth = w_mod.shape[0]
    hd = attn_subln.shape[-1]
    dh = attn_lambda.shape[-1]
    g, p, h = s5_b_re.shape[2:]
    sw = g * h
    t = S5_CHUNK
    assert batch + 1 <= SUBLANES and conv_w.shape[-1] == sw and mix_w_in.shape[-1] == 4 * sw
    assert g % S5_BLOCK == 0 and (S5_BLOCK * p) % LANES == 0 and (S5_BLOCK * h * t) % LANES == 0
    assert seq % (t * SUBLANES) == 0 and n_ctx % (t * SUBLANES) == 0 and seq % n_ctx == 0 and hd == 2 * dh
    tm = _pick(math.gcd(seq, batch * n_ctx), (512, 256, 128, 64, 32, 16))
    lay = _Layout(batch, seq, n_ctx, tm, dh ** -0.5 * math.log2(math.e))
    r = batch * (seq + n_ctx)

    xt = jnp.concatenate([x.reshape(batch * seq, d), ctx.reshape(batch * n_ctx, d)], axis=0)
    cc = jnp.zeros((SUBLANES, d), F32).at[:batch].set(c).at[batch].set(c_ctx)
    mod = _ada(cc, w_mod, b_mod).reshape(depth, SUBLANES, 1, N_MOD * d)

    wg, wu, wd = ffn_wg.astype(BF16), ffn_wu.astype(BF16), ffn_wd.astype(BF16)
    w_in, w_out, w_glu = mix_w_in.astype(BF16), mix_w_out.astype(BF16), s5_w_glu.astype(BF16)
    w_qkv, w_o = attn_w_qkv.astype(BF16), attn_w_o.astype(BF16)

    cos, sin = _rope_tables(seq)
    pos = jnp.concatenate([jnp.tile(jnp.arange(seq), batch), jnp.tile(jnp.arange(n_ctx), batch)])
    last = jnp.concatenate([jnp.full((batch * seq,), seq - 1), jnp.full((batch * n_ctx,), n_ctx - 1)])
    keep_prev = (pos != 0).astype(F32)[:, None]
    keep_next = (pos != last).astype(F32)[:, None]
    nc = r // t
    mats = jax.vmap(functools.partial(_s5_matrices, t=t))(s5_lam_re, s5_lam_im, s5_log_step, s5_b_re, s5_b_im,
                                                          s5_c_re, s5_c_im)

    for layer in range(depth):
        i = layer // 2
        xt = _ffn(xt, mod, norm_g, wg, wu, wd, layer, 0, 0, lay)
        if layer % 2 == 0:
            pj = _proj(xt, mod, norm_g, w_in, i, layer, lay)
            nb, wb = g // S5_BLOCK, S5_BLOCK * h
            uf = pj[:, :sw].reshape(nc, t, nb, wb).transpose(2, 0, 1, 3).reshape(nb, nc, t * wb)
            ys = _s5(uf, mats, i, lay)
            ys = ys.reshape(nb, nc, t, wb).transpose(1, 2, 0, 3).reshape(r, sw)
            xt = _mixout(xt, mod, norm_g, ys, pj, keep_prev, keep_next, s5_d, w_glu, s5_b_glu, conv_w, conv_b, w_out,
                         layer, i, lay)
        else:
            lam_init = 0.8 - 0.6 * math.exp(-0.3 * layer)
            lv = attn_lambda[i].astype(F32)
            lam = (jnp.exp(jnp.sum(lv[0] * lv[1])) - jnp.exp(jnp.sum(lv[2] * lv[3])) + lam_init).reshape(1)
            qkv = _proj(xt, mod, norm_g, w_qkv, i, layer, lay, rope=(cos, sin, 2 * d, d))
            a = _attn(qkv, lam, attn_subln[i], lam_init, lay)
            xt = _attnout(xt, mod, norm_g, a, w_o, layer, i, lay)
        xt = _ffn(xt, mod, norm_g, wg, wu, wd, layer, 2, 1, lay, rows=batch * seq if layer == depth - 1 else None)
    return xt.reshape(batch, seq, d)
```

```python
import functools
import math

import jax
import jax.numpy as jnp
from jax import lax
from jax.experimental import pallas as pl
from jax.experimental.pallas import tpu as pltpu

F32 = jnp.float32
BF16 = jnp.bfloat16
EPS = 1e-6
N_MOD = 9
GRID_W = 64
ROPE_BASE = 10000.0
S5_CHUNK = 16
S5_BLOCK = 4
SUBLANES = 8
LANES = 128
ROW_CHUNK = 16
ROW_UNROLL = 8
VMEM_LIMIT = 56 * 1024 * 1024


def _cparams(sem):
    return pltpu.CompilerParams(dimension_semantics=sem, vmem_limit_bytes=VMEM_LIMIT)


def _dot(a, b):
    return jnp.dot(a, b, preferred_element_type=F32)


def _rms(x, g):
    return x * lax.rsqrt(jnp.mean(x * x, axis=-1, keepdims=True) + EPS) * g


def _for_row_chunks(n_rows, fn):
    def body(i, carry):
        fn(pl.ds(pl.multiple_of(i * ROW_CHUNK, ROW_CHUNK), ROW_CHUNK))
        return carry
    lax.fori_loop(0, n_rows // ROW_CHUNK, body, 0, unroll=ROW_UNROLL)


def _norm_modulate(x_ref, h_ref, g, scale, shift):
    gs = g * (1.0 + scale)

    def chunk(rows):
        x = x_ref[rows, :]
        h_ref[rows, :] = (_rms(x, gs) + shift).astype(h_ref.dtype)
    _for_row_chunks(x_ref.shape[0], chunk)


def _gated_residual(x_ref, y_ref, o_ref, gg):
    def chunk(rows):
        o_ref[rows, :] = x_ref[rows, :] + _rms(y_ref[rows, :], gg)
    _for_row_chunks(x_ref.shape[0], chunk)


def _pick(n, candidates):
    for c in candidates:
        if n % c == 0:
            return c
    return n


def _ada_kernel(c_ref, w_ref, b_ref, o_ref):
    s = c_ref[...]
    s = s * jax.nn.sigmoid(s)
    o_ref[...] = _dot(s.astype(BF16), w_ref[...].astype(BF16)) + b_ref[...]


def _ada(cc, w_mod, b_mod):
    depth, d, n = w_mod.shape
    tn = _pick(n, (1024, 512, 256, 128))
    return pl.pallas_call(
        _ada_kernel,
        grid=(depth, n // tn),
        in_specs=[pl.BlockSpec((SUBLANES, d), lambda l, j: (0, 0)),
                  pl.BlockSpec((None, d, tn), lambda l, j: (l, 0, j)),
                  pl.BlockSpec((None, 1, tn), lambda l, j: (l, 0, j))],
        out_specs=pl.BlockSpec((None, SUBLANES, tn), lambda l, j: (l, 0, j)),
        out_shape=jax.ShapeDtypeStruct((depth, SUBLANES, n), F32),
        compiler_params=_cparams(("arbitrary", "arbitrary")),
    )(cc, w_mod, b_mod.reshape(depth, 1, n))


def _mod3(mod_ref, sub, d):
    return tuple(mod_ref[:, (3 * sub + k) * d:(3 * sub + k + 1) * d] for k in range(3))


def _ffn_kernel(x_ref, mod_ref, g_ref, wg_ref, wu_ref, wd_ref, o_ref, h_ref, acc_ref, *, sub, d):
    j = pl.program_id(1)
    shift, scale, gate = _mod3(mod_ref, sub, d)

    @pl.when(j == 0)
    def _():
        _norm_modulate(x_ref, h_ref, g_ref[2 * sub:2 * sub + 1, :], scale, shift)
        acc_ref[...] = jnp.zeros_like(acc_ref)

    h = h_ref[...]
    a = _dot(h, wg_ref[...])
    u = _dot(h, wu_ref[...])
    act = (a * jax.nn.sigmoid(a) * u).astype(BF16)
    acc_ref[...] += _dot(act, wd_ref[...])

    @pl.when(j == pl.num_programs(1) - 1)
    def _():
        _gated_residual(x_ref, acc_ref, o_ref, 0.5 * gate * g_ref[2 * sub + 1:2 * sub + 2, :])


def _ffn(xt, mod, norm_g, wg, wu, wd, layer, sub, widx, lay, rows=None):
    d = xt.shape[1]
    r = xt.shape[0] if rows is None else rows
    f = wg.shape[-1]
    tm, tf = lay.tm, _pick(f, (512, 256, 128))
    return pl.pallas_call(
        functools.partial(_ffn_kernel, sub=sub, d=d),
        grid=(r // tm, f // tf),
        in_specs=[pl.BlockSpec((tm, d), lambda i, j: (i, 0)),
                  pl.BlockSpec((None, None, 1, N_MOD * d), lambda i, j: (layer, lay.mod_row(i), 0, 0)),
                  pl.BlockSpec((None, 6, d), lambda i, j: (layer, 0, 0)),
                  pl.BlockSpec((None, None, d, tf), lambda i, j: (layer, widx, 0, j)),
                  pl.BlockSpec((None, None, d, tf), lambda i, j: (layer, widx, 0, j)),
                  pl.BlockSpec((None, None, tf, d), lambda i, j: (layer, widx, j, 0))],
        out_specs=pl.BlockSpec((tm, d), lambda i, j: (i, 0)),
        out_shape=jax.ShapeDtypeStruct((r, d), F32),
        scratch_shapes=[pltpu.VMEM((tm, d), BF16), pltpu.VMEM((tm, d), F32)],
        compiler_params=_cparams(("parallel", "arbitrary")),
    )(xt, mod, norm_g, wg, wu, wd)


def _proj_kernel(x_ref, mod_ref, g_ref, w_ref, *rest, d, rotary):
    shift, scale, _ = _mod3(mod_ref, 1, d)
    o_ref, h_ref = rest[-2:]

    @pl.when(pl.program_id(1) == 0)
    def _():
        _norm_modulate(x_ref, h_ref, g_ref[2:3, :], scale, shift)

    y = _dot(h_ref[...], w_ref[...])
    if not rotary:
        o_ref[...] = y.astype(o_ref.dtype)
        return
    cos_ref, sin_ref = rest[:2]
    lane = lax.broadcasted_iota(jnp.int32, (1, LANES), 1)
    first_half = (lane % 64) < 32
    for c in range(y.shape[1] // LANES):
        yc = y[:, c * LANES:(c + 1) * LANES]
        partner = jnp.where(first_half, pltpu.roll(yc, LANES - 32, 1), pltpu.roll(yc, 32, 1))
        o_ref[:, c * LANES:(c + 1) * LANES] = (yc * cos_ref[...] + partner * sin_ref[...]).astype(o_ref.dtype)


def _proj(xt, mod, norm_g, w, widx, layer, lay, rope=None):
    r, d = xt.shape
    n = w.shape[-1]
    tm = lay.tm
    in_specs = [pl.BlockSpec((tm, d), lambda i, j: (i, 0)),
                pl.BlockSpec((None, None, 1, N_MOD * d), lambda i, j: (layer, lay.mod_row(i), 0, 0)),
                pl.BlockSpec((None, 6, d), lambda i, j: (layer, 0, 0))]
    if rope is None:
        tn = _pick(n, (1024, 512, 256, 128))
        tables = ()
    else:
        cos, sin, n_rot, n_q = rope
        tn = _pick(math.gcd(n_q, n), (1024, 512, 256, 128))
        tps = lay.tiles_per_seq
        one, zero = jnp.ones((tm, LANES), F32), jnp.zeros((tm, LANES), F32)
        tables = (jnp.concatenate([cos * lay.q_scale, cos, one * lay.q_scale, one]),
                  jnp.concatenate([sin * lay.q_scale, sin, zero, zero]))

        def tab_map(i, j):
            lat, pos = i < lay.n_lat_tiles, i % tps
            blk = jnp.where(j < n_q // tn, jnp.where(lat, pos, 2 * tps),
                            jnp.where(jnp.logical_and(j < n_rot // tn, lat), tps + pos, 2 * tps + 1))
            return (blk, 0)
        in_specs_tab = [pl.BlockSpec((tm, LANES), tab_map)] * 2
    in_specs.append(pl.BlockSpec((None, d, tn), lambda i, j: (widx, 0, j)))
    if rope is not None:
        in_specs += in_specs_tab
    return pl.pallas_call(
        functools.partial(_proj_kernel, d=d, rotary=rope is not None),
        grid=(r // tm, n // tn),
        in_specs=in_specs,
        out_specs=pl.BlockSpec((tm, tn), lambda i, j: (i, j)),
        out_shape=jax.ShapeDtypeStruct((r, n), BF16),
        scratch_shapes=[pltpu.VMEM((tm, d), BF16)],
        compiler_params=_cparams(("parallel", "arbitrary")),
    )(xt, mod, norm_g, w, *tables)


def _rope_tables(n):
    rows = n // GRID_W
    row = jnp.repeat(jnp.arange(rows), GRID_W).astype(F32)
    col = jnp.tile(jnp.arange(GRID_W), rows).astype(F32)
    pairs = 16
    freqs = jnp.power(ROPE_BASE, -jnp.arange(pairs, dtype=F32) / pairs)
    ang = jnp.concatenate([row[:, None] * freqs, col[:, None] * freqs], axis=-1)
    cos, sin = jnp.cos(ang), jnp.sin(ang)
    return jnp.tile(cos, (1, 4)), jnp.concatenate([-sin, sin, -sin, sin], axis=-1)


def _s5_matrices(lam_re, lam_im, log_step, b_re, b_im, c_re, c_im, t):
    hp = lax.Precision.HIGHEST
    lam_re, lam_im = lam_re.astype(F32), lam_im.astype(F32)
    dt = jnp.exp(log_step.astype(F32))[..., None]
    lr, li = lam_re * dt, lam_im * dt
    mag = jnp.exp(lr)
    a_re, a_im = mag * jnp.cos(li), mag * jnp.sin(li)
    inv = 1.0 / (lam_re * lam_re + lam_im * lam_im)
    co_re = ((a_re - 1.0) * lam_re + a_im * lam_im) * inv
    co_im = (a_im * lam_re - (a_re - 1.0) * lam_im) * inv
    b_re, b_im = b_re.astype(F32), b_im.astype(F32)
    bb_re = co_re[..., None] * b_re - co_im[..., None] * b_im
    bb_im = co_re[..., None] * b_im + co_im[..., None] * b_re
    c_re, c_im = c_re.astype(F32), c_im.astype(F32)
    tau = jnp.arange(t + 1, dtype=F32)[:, None, None, None]
    pmag = jnp.exp(tau * lr[None])
    pw_re, pw_im = pmag * jnp.cos(tau * li[None]), pmag * jnp.sin(tau * li[None])
    cp_re = c_re[None] * pw_re[:, :, :, None, :] - c_im[None] * pw_im[:, :, :, None, :]
    cp_im = c_re[None] * pw_im[:, :, :, None, :] + c_im[None] * pw_re[:, :, :, None, :]
    kern = (jnp.einsum('tdghp,dgpk->tdghk', cp_re, bb_re, precision=hp)
            - jnp.einsum('tdghp,dgpk->tdghk', cp_im, bb_im, precision=hp))
    g, p, h = b_re.shape[1:]
    s_idx = jnp.arange(t)[:, None]
    t_idx = jnp.arange(t)[None, :]

    gb = S5_BLOCK
    nb = g // gb
    th, wide = t * h, t * gb * h
    col = jnp.arange(wide)
    row = jnp.arange(th)
    grp = jnp.arange(gb)[:, None, None]
    sel_th = ((row[None, :, None] // h == col[None, None, :] // (gb * h)) & (row[None, :, None] % h == col[None, None, :] % h)
              & ((col[None, None, :] // h) % gb == grp)).astype(BF16)
    colp = jnp.arange(gb * p)
    sel_p = ((jnp.arange(p)[None, :, None] == colp[None, None, :] % p) & (colp[None, None, :] // p == grp)).astype(BF16)

    def widen(m, sel):
        return jnp.einsum('qerc,ecy->qery', m.astype(BF16).astype(F32), sel.astype(F32)).astype(BF16)

    def interleave(m):
        return m.reshape(nb, gb, t, h, m.shape[-1]).transpose(0, 2, 1, 3, 4).reshape(nb, wide, m.shape[-1])

    def toeplitz(k_dir, lag):
        m = jnp.where((lag >= 0)[:, :, None, None, None], k_dir[jnp.clip(lag, 0, t)], 0.0)
        return m.transpose(2, 0, 4, 1, 3).reshape(nb, gb, th, th)

    mk = interleave(widen(toeplitz(kern[:, 0], t_idx - s_idx) + toeplitz(kern[:, 1], s_idx - t_idx), sel_th))

    def drive(d, exps):
        wr, wi = pw_re[exps, d], pw_im[exps, d]
        re = wr[..., None] * bb_re[d][None] - wi[..., None] * bb_im[d][None]
        im = wr[..., None] * bb_im[d][None] + wi[..., None] * bb_re[d][None]
        f = lambda m: interleave(widen(m.reshape(t, nb, gb, p, h).transpose(1, 2, 0, 4, 3).reshape(nb, gb, th, p), sel_p))
        return jnp.concatenate([f(re), f(im)], axis=-1)

    def read(d, exps):
        f = lambda m: widen(m.reshape(t, nb, gb, h, p).transpose(1, 2, 4, 0, 3).reshape(nb, gb, p, th),
                            sel_th).reshape(nb, gb * p, wide)
        return jnp.concatenate([f(cp_re[exps, d]), f(-cp_im[exps, d])], axis=1)

    ar = jnp.arange(t)
    ms_f, ms_r = drive(0, t - 1 - ar), drive(1, ar)
    mo_f, mo_r = read(0, ar + 1), read(1, t - ar)
    at = jnp.stack([pw_re[t, 0], pw_im[t, 0], pw_re[t, 1], pw_im[t, 1]], axis=1)
    at = at.reshape(nb, gb, 4, p).transpose(0, 2, 1, 3).reshape(nb, 4, gb * p)
    return mk, ms_f, ms_r, mo_f, mo_r, at


def _s5_kernel(uf_ref, mk_ref, msf_ref, msr_ref, mof_ref, mor_ref, a_ref, y_ref, sf_ref, sr_ref, hf_ref, hr_ref, *,
               batch, lat_tiles, ctx_tiles):
    w = a_ref.shape[-1]
    uf = uf_ref[...]
    sf_ref[...] = _dot(uf, msf_ref[...])
    sr_ref[...] = _dot(uf, msr_ref[...])
    a = a_ref[...]
    af = (a[0:1], a[1:2])
    ab = (a[2:3], a[3:4])
    row_id = lax.broadcasted_iota(jnp.int32, (SUBLANES, 2 * w), 0)

    def tile_step(s_ref, hp_ref, tile, state, coef, rows):
        off = pl.multiple_of(tile * SUBLANES, SUBLANES)
        s = s_ref[pl.ds(off, SUBLANES), :]
        hre, him = state
        are, aim = coef
        prev = jnp.zeros((SUBLANES, 2 * w), F32)
        for i in rows:
            row = jnp.concatenate([hre, him], axis=1)
            prev = jnp.where(row_id == i, jnp.broadcast_to(row, prev.shape), prev)
            sre, sim = s[i:i + 1, :w], s[i:i + 1, w:]
            hre, him = are * hre - aim * him + sre, are * him + aim * hre + sim
        hp_ref[pl.ds(off, SUBLANES), :] = prev
        return hre, him

    zero = jnp.zeros((1, w), F32)
    state = tuple((zero, zero) for _ in range(2 * batch))
    asc, desc = tuple(range(SUBLANES)), tuple(reversed(range(SUBLANES)))

    def phase(first_tile, n_tiles, state):
        def body(k, st):
            out = []
            for b in range(batch):
                out.append(tile_step(sf_ref, hf_ref, first_tile(b) + k, st[2 * b], af, asc))
                out.append(tile_step(sr_ref, hr_ref, first_tile(b) + n_tiles - 1 - k, st[2 * b + 1], ab, desc))
            return tuple(out)
        return lax.fori_loop(0, n_tiles, body, state)

    state = phase(lambda b: batch * lat_tiles + b * ctx_tiles, ctx_tiles, state)
    phase(lambda b: b * lat_tiles, lat_tiles, state)

    y_ref[...] = (_dot(uf, mk_ref[...]) + _dot(hf_ref[...].astype(BF16), mof_ref[...])
                  + _dot(hr_ref[...].astype(BF16), mor_ref[...]))


def _s5(uf, mats, i_even, lay):
    mk, ms_f, ms_r, mo_f, mo_r, at = mats
    nb, nc, wu = uf.shape
    ws = ms_f.shape[-1]
    t = S5_CHUNK
    full = lambda *shape: pl.BlockSpec((None,) + shape, lambda q: (q,) + (0,) * len(shape))
    mat = lambda *shape: pl.BlockSpec((None, None) + shape, lambda q: (i_even, q) + (0,) * len(shape))
    return pl.pallas_call(
        functools.partial(_s5_kernel, batch=lay.batch, lat_tiles=lay.seq // (t * SUBLANES),
                          ctx_tiles=lay.ctx // (t * SUBLANES)),
        grid=(nb,),
        in_specs=[full(nc, wu), mat(wu, wu), mat(wu, ws), mat(wu, ws), mat(ws, wu), mat(ws, wu),
                  mat(4, ws // 2)],
        out_specs=full(nc, wu),
        out_shape=jax.ShapeDtypeStruct((nb, nc, wu), F32),
        scratch_shapes=[pltpu.VMEM((nc, ws), F32)] * 4,
        compiler_params=_cparams(("parallel",)),
    )(uf, mk, ms_f, ms_r, mo_f, mo_r, at)


def _gelu_tanh(x):
    return 0.5 * x * (1.0 + jnp.tanh(math.sqrt(2.0 / math.pi) * (x + 0.044715 * (x * x * x))))


def _mixout_kernel(x_ref, mod_ref, g_ref, y_ref, u_ref, bg_ref, cg_ref, v_ref, cgp_ref, vp_ref, cgn_ref, vn_ref,
                   kp_ref, kn_ref, d_ref, wglu_ref, bglu_ref, cw_ref, cb_ref, wout_ref, o_ref, *, d, halo):
    _, _, gate = _mod3(mod_ref, 1, d)
    sw = u_ref.shape[1]
    tm = x_ref.shape[0]
    z = _gelu_tanh(y_ref[...] + d_ref[...] * u_ref[...].astype(F32))
    s5 = z * jax.nn.sigmoid(_dot(z.astype(BF16), wglu_ref[...]) + bglu_ref[...])
    gv = cg_ref[...].astype(F32) * v_ref[...].astype(F32)
    gv_before = cgp_ref[halo - 1:halo, :].astype(F32) * vp_ref[halo - 1:halo, :].astype(F32)
    gv_after = cgn_ref[0:1, :].astype(F32) * vn_ref[0:1, :].astype(F32)
    row = lax.broadcasted_iota(jnp.int32, (tm, 1), 0)
    prev = jnp.where(row == 0, gv_before, pltpu.roll(gv, 1, 0)) * kp_ref[...]
    nxt = jnp.where(row == tm - 1, gv_after, pltpu.roll(gv, tm - 1, 0)) * kn_ref[...]
    conv = cw_ref[0:1, :] * prev + cw_ref[1:2, :] * gv + cw_ref[2:3, :] * nxt + cb_ref[...]
    conv = bg_ref[...].astype(F32) * conv
    y = _dot(s5.astype(BF16), wout_ref[:sw, :]) + _dot(conv.astype(BF16), wout_ref[sw:, :])
    o_ref[...] = x_ref[...] + gate * _rms(y, g_ref[3:4, :])


def _mixout(xt, mod, norm_g, ys, p, keep_prev, keep_next, s5_d, w_glu, b_glu, conv_w, conv_b, w_out, layer, i_even,
            lay):
    r, d = xt.shape
    sw = ys.shape[1]
    tm = min(lay.tm, 256)
    halo = 16
    hb = tm // halo
    last_halo = r // halo - 1
    col = lambda c: pl.BlockSpec((tm, sw), lambda i: (i, c))
    before = lambda c: pl.BlockSpec((halo, sw), lambda i: (jnp.maximum(i * hb - 1, 0), c))
    after = lambda c: pl.BlockSpec((halo, sw), lambda i: (jnp.minimum((i + 1) * hb, last_halo), c))
    vec = lambda n: pl.BlockSpec((None, 1, n), lambda i: (i_even, 0, 0))
    lay_m = lay.with_tm(tm)
    return pl.pallas_call(
        functools.partial(_mixout_kernel, d=d, halo=halo),
        grid=(r // tm,),
        in_specs=[pl.BlockSpec((tm, d), lambda i: (i, 0)),
                  pl.BlockSpec((None, None, 1, N_MOD * d), lambda i: (layer, lay_m.mod_row(i), 0, 0)),
                  pl.BlockSpec((None, 6, d), lambda i: (layer, 0, 0)),
                  pl.BlockSpec((tm, sw), lambda i: (i, 0)),
                  col(0), col(1), col(2), col(3), before(2), before(3), after(2), after(3),
                  pl.BlockSpec((tm, 1), lambda i: (i, 0)),
                  pl.BlockSpec((tm, 1), lambda i: (i, 0)),
                  vec(sw),
                  pl.BlockSpec((None, sw, sw), lambda i: (i_even, 0, 0)),
                  vec(sw),
                  pl.BlockSpec((None, 3, sw), lambda i: (i_even, 0, 0)),
                  vec(sw),
                  pl.BlockSpec((None, 2 * sw, d), lambda i: (i_even, 0, 0))],
        out_specs=pl.BlockSpec((tm, d), lambda i: (i, 0)),
        out_shape=jax.ShapeDtypeStruct((r, d), F32),
        compiler_params=_cparams(("parallel",)),
    )(xt, mod, norm_g, ys, p, p, p, p, p, p, p, p, keep_prev, keep_next,
      s5_d.reshape(s5_d.shape[0], 1, sw), w_glu, b_glu.reshape(b_glu.shape[0], 1, sw), conv_w,
      conv_b.reshape(conv_b.shape[0], 1, sw), w_out)


def _attn_kernel(lam_ref, q_ref, kc_ref, vc_ref, k_ref, v_ref, sub_ref, o_ref, q2_ref, s_ref, m_ref, al_ref, l_ref,
                 acc_ref, *, n_lat_q, tk, out_scale):
    qi = pl.program_id(2)
    tq, hd = q_ref.shape
    qt = q_ref[...].astype(F32).T
    row = lax.broadcasted_iota(jnp.int32, (hd, tq), 0)
    q2_ref[:, :tq] = jnp.where(row < hd // 2, qt, 0.0).astype(BF16)
    q2_ref[:, tq:] = jnp.where(row >= hd // 2, qt, 0.0).astype(BF16)

    def weighted_values(v, pr):
        return lax.dot_general(v, pr.astype(BF16), (((0,), (0,)), ((), ())), preferred_element_type=F32)

    def scores(c):
        return _dot(k_ref[c * tk:(c + 1) * tk, :], q2_ref[...])

    s_ref[0] = scores(0)

    s = _dot(kc_ref[...], q2_ref[...])
    m0 = jnp.max(s, axis=0, keepdims=True)
    pr = jnp.exp2(s - m0)
    m_ref[...] = m0
    l_ref[...] = jnp.sum(pr, axis=0, keepdims=True)
    acc_ref[...] = weighted_values(vc_ref[...], pr)

    @pl.when(qi < n_lat_q)
    def _():
        n_chunks = k_ref.shape[0] // tk

        def fold_max(buf):
            m_prev = m_ref[...]
            m_new = jnp.maximum(m_prev, jnp.max(s_ref[buf], axis=0, keepdims=True))
            al_ref[...] = jnp.exp2(m_prev - m_new)
            m_ref[...] = m_new

        fold_max(0)
        for c in range(n_chunks):
            cur, nxt, more = c % 2, (c + 1) % 2, c + 1 < n_chunks
            if more:
                s_ref[nxt] = scores(c + 1)
            alpha = al_ref[...]
            pr = jnp.exp2(s_ref[cur] - m_ref[...])
            l_ref[...] = alpha * l_ref[...] + jnp.sum(pr, axis=0, keepdims=True)
            pv = weighted_values(v_ref[c * tk:(c + 1) * tk, :], pr)
            if more:
                fold_max(nxt)
            acc_ref[...] = alpha * acc_ref[...] + pv

    o = acc_ref[...] * (1.0 / l_ref[...])
    o = o[:, :tq] - lam_ref[0] * o[:, tq:]
    o = o * lax.rsqrt(jnp.mean(o * o, axis=0, keepdims=True) + EPS)
    o = o * jnp.tile(sub_ref[...], (1, tq // LANES)) * out_scale
    o_ref[...] = o.T.astype(o_ref.dtype)


def _attn(qkv, lam, subln, lam_init, lay):
    r = qkv.shape[0]
    hd = subln.shape[-1]
    heads = qkv.shape[1] // (3 * hd)
    d = heads * hd
    tq = lay.ctx
    tk = _pick(lay.seq, (512, 256, 128))
    n_lat_q = lay.seq // tq
    lat_blocks = lay.batch * n_lat_q

    def q_block(b, qi):
        return jnp.where(qi < n_lat_q, b * n_lat_q + qi, lat_blocks + b)

    return pl.pallas_call(
        functools.partial(_attn_kernel, n_lat_q=n_lat_q, tk=tk, out_scale=1.0 - lam_init),
        grid=(lay.batch, heads, n_lat_q + 1),
        in_specs=[pl.BlockSpec(memory_space=pltpu.SMEM),
                  pl.BlockSpec((tq, hd), lambda b, h, qi: (q_block(b, qi), h)),
                  pl.BlockSpec((lay.ctx, hd), lambda b, h, qi: (lat_blocks + b, heads + h)),
                  pl.BlockSpec((lay.ctx, hd), lambda b, h, qi: (lat_blocks + b, 2 * heads + h)),
                  pl.BlockSpec((lay.seq, hd), lambda b, h, qi: (b, heads + h)),
                  pl.BlockSpec((lay.seq, hd), lambda b, h, qi: (b, 2 * heads + h)),
                  pl.BlockSpec((hd, LANES), lambda b, h, qi: (0, 0))],
        out_specs=pl.BlockSpec((tq, hd), lambda b, h, qi: (q_block(b, qi), h)),
        out_shape=jax.ShapeDtypeStruct((r, d), BF16),
        scratch_shapes=[pltpu.VMEM((hd, 2 * tq), BF16), pltpu.VMEM((2, tk, 2 * tq), F32)]
        + [pltpu.VMEM((1, 2 * tq), F32)] * 3 + [pltpu.VMEM((hd, 2 * tq), F32)],
        compiler_params=_cparams(("parallel", "parallel", "arbitrary")),
    )(lam, qkv, qkv, qkv, qkv, qkv, jnp.broadcast_to(subln.reshape(hd, 1), (hd, LANES)))


def _attnout_kernel(x_ref, mod_ref, g_ref, a_ref, w_ref, o_ref, *, d):
    _, _, gate = _mod3(mod_ref, 1, d)
    y = _dot(a_ref[...], w_ref[...])
    o_ref[...] = x_ref[...] + gate * _rms(y, g_ref[3:4, :])


def _attnout(xt, mod, norm_g, a, w_o, layer, i_odd, lay):
    r, d = xt.shape
    tm = min(lay.tm, 256)
    lay_m = lay.with_tm(tm)
    return pl.pallas_call(
        functools.partial(_attnout_kernel, d=d),
        grid=(r // tm,),
        in_specs=[pl.BlockSpec((tm, d), lambda i: (i, 0)),
                  pl.BlockSpec((None, None, 1, N_MOD * d), lambda i: (layer, lay_m.mod_row(i), 0, 0)),
                  pl.BlockSpec((None, 6, d), lambda i: (layer, 0, 0)),
                  pl.BlockSpec((tm, a.shape[1]), lambda i: (i, 0)),
                  pl.BlockSpec((None, a.shape[1], d), lambda i: (i_odd, 0, 0))],
        out_specs=pl.BlockSpec((tm, d), lambda i: (i, 0)),
        out_shape=jax.ShapeDtypeStruct((r, d), F32),
        compiler_params=_cparams(("parallel",)),
    )(xt, mod, norm_g, a, w_o)


class _Layout:
    def __init__(self, batch, seq, ctx, tm, q_scale):
        self.batch, self.seq, self.ctx, self.tm, self.q_scale = batch, seq, ctx, tm, q_scale
        self.tiles_per_seq = seq // tm
        self.n_lat_tiles = batch * self.tiles_per_seq

    def with_tm(self, tm):
        return _Layout(self.batch, self.seq, self.ctx, tm, self.q_scale)

    def mod_row(self, i):
        return jnp.minimum(i // self.tiles_per_seq, self.batch)


def kernel(x, c, ctx, c_ctx, w_mod, b_mod, norm_g, ffn_wg, ffn_wu, ffn_wd, mix_w_in, mix_w_out, s5_lam_re, s5_lam_im, s5_log_step, s5_b_re, s5_b_im, s5_c_re, s5_c_im, s5_d, s5_w_glu, s5_b_glu, conv_w, conv_b, attn_w_qkv, attn_w_o, attn_lambda, attn_subln):
    batch, seq, d = x.shape
    n_ctx = ctx.shape[1]
    depth = w_mod.shape[0]
    hd = attn_subln.shape[-1]
    dh = attn_lambda.shape[-1]
    g, p, h = s5_b_re.shape[2:]
    sw = g * h
    t = S5_CHUNK
    assert batch + 1 <= SUBLANES and conv_w.shape[-1] == sw and mix_w_in.shape[-1] == 4 * sw
    assert g % S5_BLOCK == 0 and (S5_BLOCK * p) % LANES == 0 and (S5_BLOCK * h * t) % LANES == 0
    assert seq % (t * SUBLANES) == 0 and n_ctx % (t * SUBLANES) == 0 and seq % n_ctx == 0 and hd == 2 * dh
    tm = _pick(math.gcd(seq, batch * n_ctx), (512, 256, 128, 64, 32, 16))
    lay = _Layout(batch, seq, n_ctx, tm, dh ** -0.5 * math.log2(math.e))
    r = batch * (seq + n_ctx)

    xt = jnp.concatenate([x.reshape(batch * seq, d), ctx.reshape(batch * n_ctx, d)], axis=0)
    cc = jnp.zeros((SUBLANES, d), F32).at[:batch].set(c).at[batch].set(c_ctx)
    mod = _ada(cc, w_mod, b_mod).reshape(depth, SUBLANES, 1, N_MOD * d)

    wg, wu, wd = ffn_wg.astype(BF16), ffn_wu.astype(BF16), ffn_wd.astype(BF16)
    w_in, w_out, w_glu = mix_w_in.astype(BF16), mix_w_out.astype(BF16), s5_w_glu.astype(BF16)
    w_qkv, w_o = attn_w_qkv.astype(BF16), attn_w_o.astype(BF16)

    cos, sin = _rope_tables(seq)
    pos = jnp.concatenate([jnp.tile(jnp.arange(seq), batch), jnp.tile(jnp.arange(n_ctx), batch)])
    last = jnp.concatenate([jnp.full((batch * seq,), seq - 1), jnp.full((batch * n_ctx,), n_ctx - 1)])
    keep_prev = (pos != 0).astype(F32)[:, None]
    keep_next = (pos != last).astype(F32)[:, None]
    nc = r // t
    mats = jax.vmap(functools.partial(_s5_matrices, t=t))(s5_lam_re, s5_lam_im, s5_log_step, s5_b_re, s5_b_im,
                                                          s5_c_re, s5_c_im)

    for layer in range(depth):
        i = layer // 2
        xt = _ffn(xt, mod, norm_g, wg, wu, wd, layer, 0, 0, lay)
        if layer % 2 == 0:
            pj = _proj(xt, mod, norm_g, w_in, i, layer, lay)
            nb, wb = g // S5_BLOCK, S5_BLOCK * h
            uf = pj[:, :sw].reshape(nc, t, nb, wb).transpose(2, 0, 1, 3).reshape(nb, nc, t * wb)
            ys = _s5(uf, mats, i, lay)
            ys = ys.reshape(nb, nc, t, wb).transpose(1, 2, 0, 3).reshape(r, sw)
            xt = _mixout(xt, mod, norm_g, ys, pj, keep_prev, keep_next, s5_d, w_glu, s5_b_glu, conv_w, conv_b, w_out,
                         layer, i, lay)
        else:
            lam_init = 0.8 - 0.6 * math.exp(-0.3 * layer)
            lv = attn_lambda[i].astype(F32)
            lam = (jnp.exp(jnp.sum(lv[0] * lv[1])) - jnp.exp(jnp.sum(lv[2] * lv[3])) + lam_init).reshape(1)
            qkv = _proj(xt, mod, norm_g, w_qkv, i, layer, lay, rope=(cos, sin, 2 * d, d))
            a = _attn(qkv, lam, attn_subln[i], lam_init, lay)
            xt = _attnout(xt, mod, norm_g, a, w_o, layer, i, lay)
        xt = _ffn(xt, mod, norm_g, wg, wu, wd, layer, 2, 1, lay, rows=batch * seq if layer == depth - 1 else None)
    return xt.reshape(batch, seq, d)
```

```python
import functools
import math

import jax
import jax.numpy as jnp
from jax import lax
from jax.experimental import pallas as pl
from jax.experimental.pallas import tpu as pltpu

F32 = jnp.float32
BF16 = jnp.bfloat16
EPS = 1e-6
N_MOD = 9
GRID_W = 64
ROPE_BASE = 10000.0
S5_CHUNK = 16
S5_BLOCK = 4
SUBLANES = 8
LANES = 128
ROW_CHUNK = 16
ROW_UNROLL = 8
VMEM_LIMIT = 56 * 1024 * 1024


def _cparams(sem):
    return pltpu.CompilerParams(dimension_semantics=sem, vmem_limit_bytes=VMEM_LIMIT)


def _dot(a, b):
    return jnp.dot(a, b, preferred_element_type=F32)


def _rms(x, g):
    return x * lax.rsqrt(jnp.mean(x * x, axis=-1, keepdims=True) + EPS) * g


def _for_row_chunks(n_rows, fn):
    def body(i, carry):
        fn(pl.ds(pl.multiple_of(i * ROW_CHUNK, ROW_CHUNK), ROW_CHUNK))
        return carry
    lax.fori_loop(0, n_rows // ROW_CHUNK, body, 0, unroll=ROW_UNROLL)


def _norm_modulate(x_ref, h_ref, g, scale, shift):
    gs = g * (1.0 + scale)

    def chunk(rows):
        x = x_ref[rows, :]
        h_ref[rows, :] = (_rms(x, gs) + shift).astype(h_ref.dtype)
    _for_row_chunks(x_ref.shape[0], chunk)


def _gated_residual(x_ref, y_ref, o_ref, gg):
    def chunk(rows):
        o_ref[rows, :] = x_ref[rows, :] + _rms(y_ref[rows, :], gg)
    _for_row_chunks(x_ref.shape[0], chunk)


def _pick(n, candidates):
    for c in candidates:
        if n % c == 0:
            return c
    return n


def _ada_kernel(c_ref, w_ref, b_ref, o_ref):
    s = c_ref[...]
    s = s * jax.nn.sigmoid(s)
    o_ref[...] = _dot(s.astype(BF16), w_ref[...].astype(BF16)) + b_ref[...]


def _ada(cc, w_mod, b_mod):
    depth, d, n = w_mod.shape
    tn = _pick(n, (1024, 512, 256, 128))
    return pl.pallas_call(
        _ada_kernel,
        grid=(depth, n // tn),
        in_specs=[pl.BlockSpec((SUBLANES, d), lambda l, j: (0, 0)),
                  pl.BlockSpec((None, d, tn), lambda l, j: (l, 0, j)),
                  pl.BlockSpec((None, 1, tn), lambda l, j: (l, 0, j))],
        out_specs=pl.BlockSpec((None, SUBLANES, tn), lambda l, j: (l, 0, j)),
        out_shape=jax.ShapeDtypeStruct((depth, SUBLANES, n), F32),
        compiler_params=_cparams(("arbitrary", "arbitrary")),
    )(cc, w_mod, b_mod.reshape(depth, 1, n))


def _mod3(mod_ref, sub, d):
    return tuple(mod_ref[:, (3 * sub + k) * d:(3 * sub + k + 1) * d] for k in range(3))


def _ffn_kernel(x_ref, mod_ref, g_ref, wg_ref, wu_ref, wd_ref, o_ref, h_ref, acc_ref, *, sub, d):
    j = pl.program_id(1)
    shift, scale, gate = _mod3(mod_ref, sub, d)

    @pl.when(j == 0)
    def _():
        _norm_modulate(x_ref, h_ref, g_ref[2 * sub:2 * sub + 1, :], scale, shift)
        acc_ref[...] = jnp.zeros_like(acc_ref)

    h = h_ref[...]
    a = _dot(h, wg_ref[...])
    u = _dot(h, wu_ref[...])
    act = (a * jax.nn.sigmoid(a) * u).astype(BF16)
    acc_ref[...] += _dot(act, wd_ref[...])

    @pl.when(j == pl.num_programs(1) - 1)
    def _():
        _gated_residual(x_ref, acc_ref, o_ref, 0.5 * gate * g_ref[2 * sub + 1:2 * sub + 2, :])


def _ffn(xt, mod, norm_g, wg, wu, wd, layer, sub, widx, lay, rows=None):
    d = xt.shape[1]
    r = xt.shape[0] if rows is None else rows
    f = wg.shape[-1]
    tm, tf = lay.tm, _pick(f, (512, 256, 128))
    return pl.pallas_call(
        functools.partial(_ffn_kernel, sub=sub, d=d),
        grid=(r // tm, f // tf),
        in_specs=[pl.BlockSpec((tm, d), lambda i, j: (i, 0)),
                  pl.BlockSpec((None, None, 1, N_MOD * d), lambda i, j: (layer, lay.mod_row(i), 0, 0)),
                  pl.BlockSpec((None, 6, d), lambda i, j: (layer, 0, 0)),
                  pl.BlockSpec((None, None, d, tf), lambda i, j: (layer, widx, 0, j)),
                  pl.BlockSpec((None, None, d, tf), lambda i, j: (layer, widx, 0, j)),
                  pl.BlockSpec((None, None, tf, d), lambda i, j: (layer, widx, j, 0))],
        out_specs=pl.BlockSpec((tm, d), lambda i, j: (i, 0)),
        out_shape=jax.ShapeDtypeStruct((r, d), F32),
        scratch_shapes=[pltpu.VMEM((tm, d), BF16), pltpu.VMEM((tm, d), F32)],
        compiler_params=_cparams(("parallel", "arbitrary")),
    )(xt, mod, norm_g, wg, wu, wd)


def _proj_kernel(x_ref, mod_ref, g_ref, w_ref, *rest, d, rotary):
    shift, scale, _ = _mod3(mod_ref, 1, d)
    o_ref, h_ref = rest[-2:]

    @pl.when(pl.program_id(1) == 0)
    def _():
        _norm_modulate(x_ref, h_ref, g_ref[2:3, :], scale, shift)

    y = _dot(h_ref[...], w_ref[...])
    if not rotary:
        o_ref[...] = y.astype(o_ref.dtype)
        return
    cos_ref, sin_ref = rest[:2]
    lane = lax.broadcasted_iota(jnp.int32, (1, LANES), 1)
    first_half = (lane % 64) < 32
    for c in range(y.shape[1] // LANES):
        yc = y[:, c * LANES:(c + 1) * LANES]
        partner = jnp.where(first_half, pltpu.roll(yc, LANES - 32, 1), pltpu.roll(yc, 32, 1))
        o_ref[:, c * LANES:(c + 1) * LANES] = (yc * cos_ref[...] + partner * sin_ref[...]).astype(o_ref.dtype)


def _proj(xt, mod, norm_g, w, widx, layer, lay, rope=None):
    r, d = xt.shape
    n = w.shape[-1]
    tm = lay.tm
    in_specs = [pl.BlockSpec((tm, d), lambda i, j: (i, 0)),
                pl.BlockSpec((None, None, 1, N_MOD * d), lambda i, j: (layer, lay.mod_row(i), 0, 0)),
                pl.BlockSpec((None, 6, d), lambda i, j: (layer, 0, 0))]
    if rope is None:
        tn = _pick(n, (1024, 512, 256, 128))
        tables = ()
    else:
        cos, sin, n_rot, n_q = rope
        tn = _pick(math.gcd(n_q, n), (1024, 512, 256, 128))
        tps = lay.tiles_per_seq
        one, zero = jnp.ones((tm, LANES), F32), jnp.zeros((tm, LANES), F32)
        tables = (jnp.concatenate([cos * lay.q_scale, cos, one * lay.q_scale, one]),
                  jnp.concatenate([sin * lay.q_scale, sin, zero, zero]))

        def tab_map(i, j):
            lat, pos = i < lay.n_lat_tiles, i % tps
            blk = jnp.where(j < n_q // tn, jnp.where(lat, pos, 2 * tps),
                            jnp.where(jnp.logical_and(j < n_rot // tn, lat), tps + pos, 2 * tps + 1))
            return (blk, 0)
        in_specs_tab = [pl.BlockSpec((tm, LANES), tab_map)] * 2
    in_specs.append(pl.BlockSpec((None, d, tn), lambda i, j: (widx, 0, j)))
    if rope is not None:
        in_specs += in_specs_tab
    return pl.pallas_call(
        functools.partial(_proj_kernel, d=d, rotary=rope is not None),
        grid=(r // tm, n // tn),
        in_specs=in_specs,
        out_specs=pl.BlockSpec((tm, tn), lambda i, j: (i, j)),
        out_shape=jax.ShapeDtypeStruct((r, n), BF16),
        scratch_shapes=[pltpu.VMEM((tm, d), BF16)],
        compiler_params=_cparams(("parallel", "arbitrary")),
    )(xt, mod, norm_g, w, *tables)


def _rope_tables(n):
    rows = n // GRID_W
    row = jnp.repeat(jnp.arange(rows), GRID_W).astype(F32)
    col = jnp.tile(jnp.arange(GRID_W), rows).astype(F32)
    pairs = 16
    freqs = jnp.power(ROPE_BASE, -jnp.arange(pairs, dtype=F32) / pairs)
    ang = jnp.concatenate([row[:, None] * freqs, col[:, None] * freqs], axis=-1)
    cos, sin = jnp.cos(ang), jnp.sin(ang)
    return jnp.tile(cos, (1, 4)), jnp.concatenate([-sin, sin, -sin, sin], axis=-1)


def _s5_matrices(lam_re, lam_im, log_step, b_re, b_im, c_re, c_im, t):
    hp = lax.Precision.HIGHEST
    lam_re, lam_im = lam_re.astype(F32), lam_im.astype(F32)
    dt = jnp.exp(log_step.astype(F32))[..., None]
    lr, li = lam_re * dt, lam_im * dt
    mag = jnp.exp(lr)
    a_re, a_im = mag * jnp.cos(li), mag * jnp.sin(li)
    inv = 1.0 / (lam_re * lam_re + lam_im * lam_im)
    co_re = ((a_re - 1.0) * lam_re + a_im * lam_im) * inv
    co_im = (a_im * lam_re - (a_re - 1.0) * lam_im) * inv
    b_re, b_im = b_re.astype(F32), b_im.astype(F32)
    bb_re = co_re[..., None] * b_re - co_im[..., None] * b_im
    bb_im = co_re[..., None] * b_im + co_im[..., None] * b_re
    c_re, c_im = c_re.astype(F32), c_im.astype(F32)
    tau = jnp.arange(t + 1, dtype=F32)[:, None, None, None]
    pmag = jnp.exp(tau * lr[None])
    pw_re, pw_im = pmag * jnp.cos(tau * li[None]), pmag * jnp.sin(tau * li[None])
    cp_re = c_re[None] * pw_re[:, :, :, None, :] - c_im[None] * pw_im[:, :, :, None, :]
    cp_im = c_re[None] * pw_im[:, :, :, None, :] + c_im[None] * pw_re[:, :, :, None, :]
    kern = (jnp.einsum('tdghp,dgpk->tdghk', cp_re, bb_re, precision=hp)
            - jnp.einsum('tdghp,dgpk->tdghk', cp_im, bb_im, precision=hp))
    g, p, h = b_re.shape[1:]
    s_idx = jnp.arange(t)[:, None]
    t_idx = jnp.arange(t)[None, :]

    gb = S5_BLOCK
    nb = g // gb
    th, wide = t * h, t * gb * h
    col = jnp.arange(wide)
    row = jnp.arange(th)
    grp = jnp.arange(gb)[:, None, None]
    sel_th = ((row[None, :, None] // h == col[None, None, :] // (gb * h)) & (row[None, :, None] % h == col[None, None, :] % h)
              & ((col[None, None, :] // h) % gb == grp)).astype(BF16)
    colp = jnp.arange(gb * p)
    sel_p = ((jnp.arange(p)[None, :, None] == colp[None, None, :] % p) & (colp[None, None, :] // p == grp)).astype(BF16)

    def widen(m, sel):
        return jnp.einsum('qerc,ecy->qery', m.astype(BF16).astype(F32), sel.astype(F32)).astype(BF16)

    def interleave(m):
        return m.reshape(nb, gb, t, h, m.shape[-1]).transpose(0, 2, 1, 3, 4).reshape(nb, wide, m.shape[-1])

    def toeplitz(k_dir, lag):
        m = jnp.where((lag >= 0)[:, :, None, None, None], k_dir[jnp.clip(lag, 0, t)], 0.0)
        return m.transpose(2, 0, 4, 1, 3).reshape(nb, gb, th, th)

    mk = interleave(widen(toeplitz(kern[:, 0], t_idx - s_idx) + toeplitz(kern[:, 1], s_idx - t_idx), sel_th))

    def drive(d, exps):
        wr, wi = pw_re[exps, d], pw_im[exps, d]
        re = wr[..., None] * bb_re[d][None] - wi[..., None] * bb_im[d][None]
        im = wr[..., None] * bb_im[d][None] + wi[..., None] * bb_re[d][None]
        f = lambda m: interleave(widen(m.reshape(t, nb, gb, p, h).transpose(1, 2, 0, 4, 3).reshape(nb, gb, th, p), sel_p))
        return jnp.concatenate([f(re), f(im)], axis=-1)

    def read(d, exps):
        f = lambda m: widen(m.reshape(t, nb, gb, h, p).transpose(1, 2, 4, 0, 3).reshape(nb, gb, p, th),
                            sel_th).reshape(nb, gb * p, wide)
        return jnp.concatenate([f(cp_re[exps, d]), f(-cp_im[exps, d])], axis=1)

    ar = jnp.arange(t)
    ms_f, ms_r = drive(0, t - 1 - ar), drive(1, ar)
    mo_f, mo_r = read(0, ar + 1), read(1, t - ar)
    at = jnp.stack([pw_re[t, 0], pw_im[t, 0], pw_re[t, 1], pw_im[t, 1]], axis=1)
    at = at.reshape(nb, gb, 4, p).transpose(0, 2, 1, 3).reshape(nb, 4, gb * p)
    return mk, ms_f, ms_r, mo_f, mo_r, at


def _s5_kernel(uf_ref, mk_ref, msf_ref, msr_ref, mof_ref, mor_ref, a_ref, y_ref, sf_ref, sr_ref, hf_ref, hr_ref, *,
               batch, lat_tiles, ctx_tiles):
    w = a_ref.shape[-1]
    uf = uf_ref[...]
    sf_ref[...] = _dot(uf, msf_ref[...])
    sr_ref[...] = _dot(uf, msr_ref[...])
    a = a_ref[...]
    af = (a[0:1], a[1:2])
    ab = (a[2:3], a[3:4])
    row_id = lax.broadcasted_iota(jnp.int32, (SUBLANES, 2 * w), 0)

    def tile_step(s_ref, hp_ref, tile, state, coef, rows):
        off = pl.multiple_of(tile * SUBLANES, SUBLANES)
        s = s_ref[pl.ds(off, SUBLANES), :]
        hre, him = state
        are, aim = coef
        prev = jnp.zeros((SUBLANES, 2 * w), F32)
        for i in rows:
            row = jnp.concatenate([hre, him], axis=1)
            prev = jnp.where(row_id == i, jnp.broadcast_to(row, prev.shape), prev)
            sre, sim = s[i:i + 1, :w], s[i:i + 1, w:]
            hre, him = are * hre - aim * him + sre, are * him + aim * hre + sim
        hp_ref[pl.ds(off, SUBLANES), :] = prev
        return hre, him

    zero = jnp.zeros((1, w), F32)
    state = tuple((zero, zero) for _ in range(2 * batch))
    asc, desc = tuple(range(SUBLANES)), tuple(reversed(range(SUBLANES)))

    def phase(first_tile, n_tiles, state):
        def body(k, st):
            out = []
            for b in range(batch):
                out.append(tile_step(sf_ref, hf_ref, first_tile(b) + k, st[2 * b], af, asc))
                out.append(tile_step(sr_ref, hr_ref, first_tile(b) + n_tiles - 1 - k, st[2 * b + 1], ab, desc))
            return tuple(out)
        return lax.fori_loop(0, n_tiles, body, state)

    state = phase(lambda b: batch * lat_tiles + b * ctx_tiles, ctx_tiles, state)
    phase(lambda b: b * lat_tiles, lat_tiles, state)

    y_ref[...] = (_dot(uf, mk_ref[...]) + _dot(hf_ref[...].astype(BF16), mof_ref[...])
                  + _dot(hr_ref[...].astype(BF16), mor_ref[...]))


def _s5(uf, mats, i_even, lay):
    mk, ms_f, ms_r, mo_f, mo_r, at = mats
    nb, nc, wu = uf.shape
    ws = ms_f.shape[-1]
    t = S5_CHUNK
    full = lambda *shape: pl.BlockSpec((None,) + shape, lambda q: (q,) + (0,) * len(shape))
    mat = lambda *shape: pl.BlockSpec((None, None) + shape, lambda q: (i_even, q) + (0,) * len(shape))
    return pl.pallas_call(
        functools.partial(_s5_kernel, batch=lay.batch, lat_tiles=lay.seq // (t * SUBLANES),
                          ctx_tiles=lay.ctx // (t * SUBLANES)),
        grid=(nb,),
        in_specs=[full(nc, wu), mat(wu, wu), mat(wu, ws), mat(wu, ws), mat(ws, wu), mat(ws, wu),
                  mat(4, ws // 2)],
        out_specs=full(nc, wu),
        out_shape=jax.ShapeDtypeStruct((nb, nc, wu), F32),
        scratch_shapes=[pltpu.VMEM((nc, ws), F32)] * 4,
        compiler_params=_cparams(("parallel",)),
    )(uf, mk, ms_f, ms_r, mo_f, mo_r, at)


def _gelu_tanh(x):
    return 0.5 * x * (1.0 + jnp.tanh(math.sqrt(2.0 / math.pi) * (x + 0.044715 * (x * x * x))))


def _mixout_kernel(x_ref, mod_ref, g_ref, y_ref, u_ref, bg_ref, cg_ref, v_ref, cgp_ref, vp_ref, cgn_ref, vn_ref,
                   kp_ref, kn_ref, d_ref, wglu_ref, bglu_ref, cw_ref, cb_ref, wout_ref, o_ref, *, d, halo):
    _, _, gate = _mod3(mod_ref, 1, d)
    sw = u_ref.shape[1]
    tm = x_ref.shape[0]
    z = _gelu_tanh(y_ref[...] + d_ref[...] * u_ref[...].astype(F32))
    s5 = z * jax.nn.sigmoid(_dot(z.astype(BF16), wglu_ref[...]) + bglu_ref[...])
    gv = cg_ref[...].astype(F32) * v_ref[...].astype(F32)
    gv_before = cgp_ref[halo - 1:halo, :].astype(F32) * vp_ref[halo - 1:halo, :].astype(F32)
    gv_after = cgn_ref[0:1, :].astype(F32) * vn_ref[0:1, :].astype(F32)
    row = lax.broadcasted_iota(jnp.int32, (tm, 1), 0)
    prev = jnp.where(row == 0, gv_before, pltpu.roll(gv, 1, 0)) * kp_ref[...]
    nxt = jnp.where(row == tm - 1, gv_after, pltpu.roll(gv, tm - 1, 0)) * kn_ref[...]
    conv = cw_ref[0:1, :] * prev + cw_ref[1:2, :] * gv + cw_ref[2:3, :] * nxt + cb_ref[...]
    conv = bg_ref[...].astype(F32) * conv
    y = _dot(s5.astype(BF16), wout_ref[:sw, :]) + _dot(conv.astype(BF16), wout_ref[sw:, :])
    o_ref[...] = x_ref[...] + gate * _rms(y, g_ref[3:4, :])


def _mixout(xt, mod, norm_g, ys, p, keep_prev, keep_next, s5_d, w_glu, b_glu, conv_w, conv_b, w_out, layer, i_even,
            lay):
    r, d = xt.shape
    sw = ys.shape[1]
    tm = min(lay.tm, 256)
    halo = 16
    hb = tm // halo
    last_halo = r // halo - 1
    col = lambda c: pl.BlockSpec((tm, sw), lambda i: (i, c))
    before = lambda c: pl.BlockSpec((halo, sw), lambda i: (jnp.maximum(i * hb - 1, 0), c))
    after = lambda c: pl.BlockSpec((halo, sw), lambda i: (jnp.minimum((i + 1) * hb, last_halo), c))
    vec = lambda n: pl.BlockSpec((None, 1, n), lambda i: (i_even, 0, 0))
    lay_m = lay.with_tm(tm)
    return pl.pallas_call(
        functools.partial(_mixout_kernel, d=d, halo=halo),
        grid=(r // tm,),
        in_specs=[pl.BlockSpec((tm, d), lambda i: (i, 0)),
                  pl.BlockSpec((None, None, 1, N_MOD * d), lambda i: (layer, lay_m.mod_row(i), 0, 0)),
                  pl.BlockSpec((None, 6, d), lambda i: (layer, 0, 0)),
                  pl.BlockSpec((tm, sw), lambda i: (i, 0)),
                  col(0), col(1), col(2), col(3), before(2), before(3), after(2), after(3),
                  pl.BlockSpec((tm, 1), lambda i: (i, 0)),
                  pl.BlockSpec((tm, 1), lambda i: (i, 0)),
                  vec(sw),
                  pl.BlockSpec((None, sw, sw), lambda i: (i_even, 0, 0)),
                  vec(sw),
                  pl.BlockSpec((None, 3, sw), lambda i: (i_even, 0, 0)),
                  vec(sw),
                  pl.BlockSpec((None, 2 * sw, d), lambda i: (i_even, 0, 0))],
        out_specs=pl.BlockSpec((tm, d), lambda i: (i, 0)),
        out_shape=jax.ShapeDtypeStruct((r, d), F32),
        compiler_params=_cparams(("parallel",)),
    )(xt, mod, norm_g, ys, p, p, p, p, p, p, p, p, keep_prev, keep_next,
      s5_d.reshape(s5_d.shape[0], 1, sw), w_glu, b_glu.reshape(b_glu.shape[0], 1, sw), conv_w,
      conv_b.reshape(conv_b.shape[0], 1, sw), w_out)


def _attn_kernel(lam_ref, q_ref, kc_ref, vct_ref, *rest, latents, tk, out_scale):
    if latents:
        k_ref, vt_ref = rest[:2]
        rest = rest[2:]
    sub_ref, o_ref, q2_ref, s_ref, m_ref, al_ref, l_ref, acc_ref = rest
    tq, hd = q_ref.shape
    qt = q_ref[...].astype(F32).T
    row = lax.broadcasted_iota(jnp.int32, (hd, tq), 0)
    q2_ref[:, :tq] = jnp.where(row < hd // 2, qt, 0.0).astype(BF16)
    q2_ref[:, tq:] = jnp.where(row >= hd // 2, qt, 0.0).astype(BF16)

    def scores(c):
        return _dot(k_ref[c * tk:(c + 1) * tk, :], q2_ref[...])

    if latents:
        s_ref[0] = scores(0)

    s = _dot(kc_ref[...], q2_ref[...])
    m0 = jnp.max(s, axis=0, keepdims=True)
    pr = jnp.exp2(s - m0)
    m_ref[...] = m0
    l_ref[...] = jnp.sum(pr, axis=0, keepdims=True)
    acc_ref[...] = _dot(vct_ref[...], pr.astype(BF16))

    def fold_max(buf):
        m_prev = m_ref[...]
        m_new = jnp.maximum(m_prev, jnp.max(s_ref[buf], axis=0, keepdims=True))
        al_ref[...] = jnp.exp2(m_prev - m_new)
        m_ref[...] = m_new

    def latent_chunks():
        n_chunks = k_ref.shape[0] // tk
        fold_max(0)
        for c in range(n_chunks):
            cur, nxt, more = c % 2, (c + 1) % 2, c + 1 < n_chunks
            if more:
                s_ref[nxt] = scores(c + 1)
            alpha = al_ref[...]
            pr = jnp.exp2(s_ref[cur] - m_ref[...])
            l_ref[...] = alpha * l_ref[...] + jnp.sum(pr, axis=0, keepdims=True)
            pv = _dot(vt_ref[:, c * tk:(c + 1) * tk], pr.astype(BF16))
            if more:
                fold_max(nxt)
            acc_ref[...] = alpha * acc_ref[...] + pv

    if latents:
        pl.when(pl.program_id(2) >= 0)(latent_chunks)

    o = acc_ref[...] * (1.0 / l_ref[...])
    o = o[:, :tq] - lam_ref[0] * o[:, tq:]
    o = o * lax.rsqrt(jnp.mean(o * o, axis=0, keepdims=True) + EPS)
    o = o * jnp.tile(sub_ref[...], (1, tq // LANES)) * out_scale
    o_ref[...] = o.T.astype(o_ref.dtype)


def _attn(qkv, lam, subln, lam_init, lay, need_ctx):
    hd = subln.shape[-1]
    heads = qkv.shape[1] // (3 * hd)
    d = heads * hd
    vt = qkv[:, 2 * d:].T
    ctx_block = lay.batch * lay.seq // lay.ctx
    sub = jnp.broadcast_to(subln.reshape(hd, 1), (hd, LANES))
    smem = pl.BlockSpec(memory_space=pltpu.SMEM)
    ctx_specs = [pl.BlockSpec((lay.ctx, hd), lambda b, h, qi: (ctx_block + b, heads + h)),
                 pl.BlockSpec((hd, lay.ctx), lambda b, h, qi: (h, ctx_block + b))]
    sub_spec = pl.BlockSpec((hd, LANES), lambda b, h, qi: (0, 0))

    def scratch(tq, tk):
        return ([pltpu.VMEM((hd, 2 * tq), BF16), pltpu.VMEM((2, tk, 2 * tq), F32)]
                + [pltpu.VMEM((1, 2 * tq), F32)] * 3 + [pltpu.VMEM((hd, 2 * tq), F32)])

    tq = _pick(lay.seq, (256, 128))
    tk = _pick(lay.seq, (512, 256, 128))
    n_q = lay.seq // tq
    a_lat = pl.pallas_call(
        functools.partial(_attn_kernel, latents=True, tk=tk, out_scale=1.0 - lam_init),
        grid=(lay.batch, heads, n_q),
        in_specs=[smem, pl.BlockSpec((tq, hd), lambda b, h, qi: (b * n_q + qi, h))] + ctx_specs
        + [pl.BlockSpec((lay.seq, hd), lambda b, h, qi: (b, heads + h)),
           pl.BlockSpec((hd, lay.seq), lambda b, h, qi: (h, b)), sub_spec],
        out_specs=pl.BlockSpec((tq, hd), lambda b, h, qi: (b * n_q + qi, h)),
        out_shape=jax.ShapeDtypeStruct((lay.batch * lay.seq, d), BF16),
        scratch_shapes=scratch(tq, tk),
        compiler_params=_cparams(("parallel", "parallel", "arbitrary")),
    )(lam, qkv, qkv, vt, qkv, vt, sub)
    if not need_ctx:
        return a_lat, None
    a_ctx = pl.pallas_call(
        functools.partial(_attn_kernel, latents=False, tk=SUBLANES, out_scale=1.0 - lam_init),
        grid=(lay.batch, heads, 1),
        in_specs=[smem, pl.BlockSpec((lay.ctx, hd), lambda b, h, qi: (ctx_block + b, h))] + ctx_specs + [sub_spec],
        out_specs=pl.BlockSpec((lay.ctx, hd), lambda b, h, qi: (b, h)),
        out_shape=jax.ShapeDtypeStruct((lay.batch * lay.ctx, d), BF16),
        scratch_shapes=scratch(lay.ctx, SUBLANES),
        compiler_params=_cparams(("parallel", "parallel", "arbitrary")),
    )(lam, qkv, qkv, vt, sub)
    return a_lat, a_ctx


def _attnout_kernel(x_ref, mod_ref, g_ref, a_ref, *rest, d, n_lat_tiles):
    _, _, gate = _mod3(mod_ref, 1, d)
    if len(rest) == 3:
        actx_ref, w_ref, o_ref = rest
        a = jnp.where(pl.program_id(0) < n_lat_tiles, a_ref[...], actx_ref[...])
    else:
        w_ref, o_ref = rest
        a = a_ref[...]
    y = _dot(a, w_ref[...])
    o_ref[...] = x_ref[...] + gate * _rms(y, g_ref[3:4, :])


def _attnout(xt, mod, norm_g, a_lat, a_ctx, w_o, layer, i_odd, lay):
    d = xt.shape[1]
    tm = min(lay.tm, 256)
    lay_m = lay.with_tm(tm)
    n_lat = lay_m.n_lat_tiles
    r = a_lat.shape[0] + (0 if a_ctx is None else a_ctx.shape[0])
    a_specs = [pl.BlockSpec((tm, d), lambda i: (jnp.minimum(i, n_lat - 1), 0))]
    a_args = [a_lat]
    if a_ctx is not None:
        a_specs.append(pl.BlockSpec((tm, d), lambda i: (jnp.maximum(i - n_lat, 0), 0)))
        a_args.append(a_ctx)
    return pl.pallas_call(
        functools.partial(_attnout_kernel, d=d, n_lat_tiles=n_lat),
        grid=(r // tm,),
        in_specs=[pl.BlockSpec((tm, d), lambda i: (i, 0)),
                  pl.BlockSpec((None, None, 1, N_MOD * d), lambda i: (layer, lay_m.mod_row(i), 0, 0)),
                  pl.BlockSpec((None, 6, d), lambda i: (layer, 0, 0))] + a_specs
        + [pl.BlockSpec((None, d, d), lambda i: (i_odd, 0, 0))],
        out_specs=pl.BlockSpec((tm, d), lambda i: (i, 0)),
        out_shape=jax.ShapeDtypeStruct((r, d), F32),
        compiler_params=_cparams(("parallel",)),
    )(xt, mod, norm_g, *a_args, w_o)


class _Layout:
    def __init__(self, batch, seq, ctx, tm, q_scale):
        self.batch, self.seq, self.ctx, self.tm, self.q_scale = batch, seq, ctx, tm, q_scale
        self.tiles_per_seq = seq // tm
        self.n_lat_tiles = batch * self.tiles_per_seq

    def with_tm(self, tm):
        return _Layout(self.batch, self.seq, self.ctx, tm, self.q_scale)

    def mod_row(self, i):
        return jnp.minimum(i // self.tiles_per_seq, self.batch)


def kernel(x, c, ctx, c_ctx, w_mod, b_mod, norm_g, ffn_wg, ffn_wu, ffn_wd, mix_w_in, mix_w_out, s5_lam_re, s5_lam_im, s5_log_step, s5_b_re, s5_b_im, s5_c_re, s5_c_im, s5_d, s5_w_glu, s5_b_glu, conv_w, conv_b, attn_w_qkv, attn_w_o, attn_lambda, attn_subln):
    batch, seq, d = x.shape
    n_ctx = ctx.shape[1]
    depth = w_mod.shape[0]
    hd = attn_subln.shape[-1]
    dh = attn_lambda.shape[-1]
    g, p, h = s5_b_re.shape[2:]
    sw = g * h
    t = S5_CHUNK
    assert batch + 1 <= SUBLANES and conv_w.shape[-1] == sw and mix_w_in.shape[-1] == 4 * sw
    assert g % S5_BLOCK == 0 and (S5_BLOCK * p) % LANES == 0 and (S5_BLOCK * h * t) % LANES == 0
    assert seq % (t * SUBLANES) == 0 and n_ctx % (t * SUBLANES) == 0 and seq % n_ctx == 0 and hd == 2 * dh
    tm = _pick(math.gcd(seq, batch * n_ctx), (512, 256, 128, 64, 32, 16))
    lay = _Layout(batch, seq, n_ctx, tm, dh ** -0.5 * math.log2(math.e))
    r = batch * (seq + n_ctx)

    xt = jnp.concatenate([x.reshape(batch * seq, d), ctx.reshape(batch * n_ctx, d)], axis=0)
    cc = jnp.zeros((SUBLANES, d), F32).at[:batch].set(c).at[batch].set(c_ctx)
    mod = _ada(cc, w_mod, b_mod).reshape(depth, SUBLANES, 1, N_MOD * d)

    wg, wu, wd = ffn_wg.astype(BF16), ffn_wu.astype(BF16), ffn_wd.astype(BF16)
    w_in, w_out, w_glu = mix_w_in.astype(BF16), mix_w_out.astype(BF16), s5_w_glu.astype(BF16)
    w_qkv, w_o = attn_w_qkv.astype(BF16), attn_w_o.astype(BF16)

    cos, sin = _rope_tables(seq)
    pos = jnp.concatenate([jnp.tile(jnp.arange(seq), batch), jnp.tile(jnp.arange(n_ctx), batch)])
    last = jnp.concatenate([jnp.full((batch * seq,), seq - 1), jnp.full((batch * n_ctx,), n_ctx - 1)])
    keep_prev = (pos != 0).astype(F32)[:, None]
    keep_next = (pos != last).astype(F32)[:, None]
    nc = r // t
    mats = jax.vmap(functools.partial(_s5_matrices, t=t))(s5_lam_re, s5_lam_im, s5_log_step, s5_b_re, s5_b_im,
                                                          s5_c_re, s5_c_im)

    for layer in range(depth):
        i = layer // 2
        xt = _ffn(xt, mod, norm_g, wg, wu, wd, layer, 0, 0, lay)
        if layer % 2 == 0:
            pj = _proj(xt, mod, norm_g, w_in, i, layer, lay)
            nb, wb = g // S5_BLOCK, S5_BLOCK * h
            uf = pj[:, :sw].reshape(nc, t, nb, wb).transpose(2, 0, 1, 3).reshape(nb, nc, t * wb)
            ys = _s5(uf, mats, i, lay)
            ys = ys.reshape(nb, nc, t, wb).transpose(1, 2, 0, 3).reshape(r, sw)
            xt = _mixout(xt, mod, norm_g, ys, pj, keep_prev, keep_next, s5_d, w_glu, s5_b_glu, conv_w, conv_b, w_out,
                         layer, i, lay)
        else:
            lam_init = 0.8 - 0.6 * math.exp(-0.3 * layer)
            lv = attn_lambda[i].astype(F32)
            lam = (jnp.exp(jnp.sum(lv[0] * lv[1])) - jnp.exp(jnp.sum(lv[2] * lv[3])) + lam_init).reshape(1)
            qkv = _proj(xt, mod, norm_g, w_qkv, i, layer, lay, rope=(cos, sin, 2 * d, d))
            a_lat, a_ctx = _attn(qkv, lam, attn_subln[i], lam_init, lay, need_ctx=layer < depth - 1)
            xt = _attnout(xt, mod, norm_g, a_lat, a_ctx, w_o, layer, i, lay)
        xt = _ffn(xt, mod, norm_g, wg, wu, wd, layer, 2, 1, lay, rows=batch * seq if layer == depth - 1 else None)
    return xt.reshape(batch, seq, d)
```

```python
import functools
import math

import jax
import jax.numpy as jnp
from jax import lax
from jax.experimental import pallas as pl
from jax.experimental.pallas import tpu as pltpu

F32 = jnp.float32
BF16 = jnp.bfloat16
EPS = 1e-6
N_MOD = 9
GRID_W = 64
ROPE_BASE = 10000.0
S5_CHUNK = 16
S5_BLOCK = 4
SUBLANES = 8
LANES = 128
ROW_CHUNK = 16
ROW_UNROLL = 8
VMEM_LIMIT = 56 * 1024 * 1024


def _cparams(sem):
    return pltpu.CompilerParams(dimension_semantics=sem, vmem_limit_bytes=VMEM_LIMIT)


def _dot(a, b):
    return jnp.dot(a, b, preferred_element_type=F32)


def _rms(x, g):
    return x * lax.rsqrt(jnp.mean(x * x, axis=-1, keepdims=True) + EPS) * g


def _for_row_chunks(n_rows, fn):
    def body(i, carry):
        fn(pl.ds(pl.multiple_of(i * ROW_CHUNK, ROW_CHUNK), ROW_CHUNK))
        return carry
    lax.fori_loop(0, n_rows // ROW_CHUNK, body, 0, unroll=ROW_UNROLL)


def _norm_modulate(x_ref, h_ref, g, scale, shift):
    gs = g * (1.0 + scale)

    def chunk(rows):
        x = x_ref[rows, :]
        h_ref[rows, :] = (_rms(x, gs) + shift).astype(h_ref.dtype)
    _for_row_chunks(x_ref.shape[0], chunk)


def _gated_residual(x_ref, y_ref, o_ref, gg):
    def chunk(rows):
        o_ref[rows, :] = x_ref[rows, :] + _rms(y_ref[rows, :], gg)
    _for_row_chunks(x_ref.shape[0], chunk)


def _pick(n, candidates):
    for c in candidates:
        if n % c == 0:
            return c
    return n


def _ada_kernel(c_ref, w_ref, b_ref, o_ref):
    s = c_ref[...]
    s = s * jax.nn.sigmoid(s)
    o_ref[...] = _dot(s.astype(BF16), w_ref[...].astype(BF16)) + b_ref[...]


def _ada(cc, w_mod, b_mod):
    depth, d, n = w_mod.shape
    tn = _pick(n, (1024, 512, 256, 128))
    return pl.pallas_call(
        _ada_kernel,
        grid=(depth, n // tn),
        in_specs=[pl.BlockSpec((SUBLANES, d), lambda l, j: (0, 0)),
                  pl.BlockSpec((None, d, tn), lambda l, j: (l, 0, j)),
                  pl.BlockSpec((None, 1, tn), lambda l, j: (l, 0, j))],
        out_specs=pl.BlockSpec((None, SUBLANES, tn), lambda l, j: (l, 0, j)),
        out_shape=jax.ShapeDtypeStruct((depth, SUBLANES, n), F32),
        compiler_params=_cparams(("arbitrary", "arbitrary")),
    )(cc, w_mod, b_mod.reshape(depth, 1, n))


def _mod3(mod_ref, sub, d):
    return tuple(mod_ref[:, (3 * sub + k) * d:(3 * sub + k + 1) * d] for k in range(3))


def _ffn_kernel(x_ref, mod_ref, g_ref, wg_ref, wu_ref, wd_ref, o_ref, h_ref, acc_ref, *, sub, d):
    j = pl.program_id(1)
    shift, scale, gate = _mod3(mod_ref, sub, d)

    @pl.when(j == 0)
    def _():
        _norm_modulate(x_ref, h_ref, g_ref[2 * sub:2 * sub + 1, :], scale, shift)
        acc_ref[...] = jnp.zeros_like(acc_ref)

    h = h_ref[...]
    a = _dot(h, wg_ref[...])
    u = _dot(h, wu_ref[...])
    act = (a * jax.nn.sigmoid(a) * u).astype(BF16)
    acc_ref[...] += _dot(act, wd_ref[...])

    @pl.when(j == pl.num_programs(1) - 1)
    def _():
        _gated_residual(x_ref, acc_ref, o_ref, 0.5 * gate * g_ref[2 * sub + 1:2 * sub + 2, :])


def _ffn(xt, mod, norm_g, wg, wu, wd, layer, sub, widx, lay, rows=None):
    d = xt.shape[1]
    r = xt.shape[0] if rows is None else rows
    f = wg.shape[-1]
    tm, tf = lay.tm, _pick(f, (512, 256, 128))
    return pl.pallas_call(
        functools.partial(_ffn_kernel, sub=sub, d=d),
        grid=(r // tm, f // tf),
        in_specs=[pl.BlockSpec((tm, d), lambda i, j: (i, 0)),
                  pl.BlockSpec((None, None, 1, N_MOD * d), lambda i, j: (layer, lay.mod_row(i), 0, 0)),
                  pl.BlockSpec((None, 6, d), lambda i, j: (layer, 0, 0)),
                  pl.BlockSpec((None, None, d, tf), lambda i, j: (layer, widx, 0, j)),
                  pl.BlockSpec((None, None, d, tf), lambda i, j: (layer, widx, 0, j)),
                  pl.BlockSpec((None, None, tf, d), lambda i, j: (layer, widx, j, 0))],
        out_specs=pl.BlockSpec((tm, d), lambda i, j: (i, 0)),
        out_shape=jax.ShapeDtypeStruct((r, d), F32),
        scratch_shapes=[pltpu.VMEM((tm, d), BF16), pltpu.VMEM((tm, d), F32)],
        compiler_params=_cparams(("parallel", "arbitrary")),
    )(xt, mod, norm_g, wg, wu, wd)


def _proj_kernel(x_ref, mod_ref, g_ref, w_ref, *rest, d, tn, n_q, n_rot):
    shift, scale, _ = _mod3(mod_ref, 1, d)
    o_ref, h_ref = rest[-2:]
    _norm_modulate(x_ref, h_ref, g_ref[2:3, :], scale, shift)
    h = h_ref[...]
    lane = lax.broadcasted_iota(jnp.int32, (1, LANES), 1)
    first_half = (lane % 64) < 32
    for c in range(w_ref.shape[1] // tn):
        col = c * tn
        y = _dot(h, w_ref[:, col:col + tn])
        if col >= n_rot:
            o_ref[:, col:col + tn] = y.astype(o_ref.dtype)
            continue
        cos_ref, sin_ref = rest[:2] if col < n_q else rest[2:4]
        for cc in range(tn // LANES):
            yc = y[:, cc * LANES:(cc + 1) * LANES]
            partner = jnp.where(first_half, pltpu.roll(yc, LANES - 32, 1), pltpu.roll(yc, 32, 1))
            o_ref[:, col + cc * LANES:col + (cc + 1) * LANES] = (
                yc * cos_ref[...] + partner * sin_ref[...]).astype(o_ref.dtype)


def _proj(xt, mod, norm_g, w, widx, layer, lay, rope=None):
    r, d = xt.shape
    n = w.shape[-1]
    tm = lay.tm
    in_specs = [pl.BlockSpec((tm, d), lambda i: (i, 0)),
                pl.BlockSpec((None, None, 1, N_MOD * d), lambda i: (layer, lay.mod_row(i), 0, 0)),
                pl.BlockSpec((None, 6, d), lambda i: (layer, 0, 0)),
                pl.BlockSpec((None, d, n), lambda i: (widx, 0, 0), pipeline_mode=pl.Buffered(1))]
    if rope is None:
        tn = _pick(n, (1024, 512, 256, 128))
        n_rot = n_q = 0
        tables = ()
    else:
        cos, sin, n_rot, n_q = rope
        tn = _pick(math.gcd(n_q, n), (1024, 512, 256, 128))
        tps = lay.tiles_per_seq
        one, zero = jnp.ones((tm, LANES), F32), jnp.zeros((tm, LANES), F32)
        cos_tab = jnp.concatenate([cos * lay.q_scale, cos, one * lay.q_scale, one])
        sin_tab = jnp.concatenate([sin * lay.q_scale, sin, zero, zero])
        q_map = lambda i: (jnp.where(i < lay.n_lat_tiles, i % tps, 2 * tps), 0)
        k_map = lambda i: (jnp.where(i < lay.n_lat_tiles, tps + i % tps, 2 * tps + 1), 0)
        tables = (cos_tab, sin_tab, cos_tab, sin_tab)
        in_specs += [pl.BlockSpec((tm, LANES), m) for m in (q_map, q_map, k_map, k_map)]
    return pl.pallas_call(
        functools.partial(_proj_kernel, d=d, tn=tn, n_q=n_q, n_rot=n_rot),
        grid=(r // tm,),
        in_specs=in_specs,
        out_specs=pl.BlockSpec((tm, n), lambda i: (i, 0)),
        out_shape=jax.ShapeDtypeStruct((r, n), BF16),
        scratch_shapes=[pltpu.VMEM((tm, d), BF16)],
        compiler_params=_cparams(("parallel",)),
    )(xt, mod, norm_g, w, *tables)


def _rope_tables(n):
    rows = n // GRID_W
    row = jnp.repeat(jnp.arange(rows), GRID_W).astype(F32)
    col = jnp.tile(jnp.arange(GRID_W), rows).astype(F32)
    pairs = 16
    freqs = jnp.power(ROPE_BASE, -jnp.arange(pairs, dtype=F32) / pairs)
    ang = jnp.concatenate([row[:, None] * freqs, col[:, None] * freqs], axis=-1)
    cos, sin = jnp.cos(ang), jnp.sin(ang)
    return jnp.tile(cos, (1, 4)), jnp.concatenate([-sin, sin, -sin, sin], axis=-1)


def _s5_matrices(lam_re, lam_im, log_step, b_re, b_im, c_re, c_im, t):
    hp = lax.Precision.HIGHEST
    lam_re, lam_im = lam_re.astype(F32), lam_im.astype(F32)
    dt = jnp.exp(log_step.astype(F32))[..., None]
    lr, li = lam_re * dt, lam_im * dt
    mag = jnp.exp(lr)
    a_re, a_im = mag * jnp.cos(li), mag * jnp.sin(li)
    inv = 1.0 / (lam_re * lam_re + lam_im * lam_im)
    co_re = ((a_re - 1.0) * lam_re + a_im * lam_im) * inv
    co_im = (a_im * lam_re - (a_re - 1.0) * lam_im) * inv
    b_re, b_im = b_re.astype(F32), b_im.astype(F32)
    bb_re = co_re[..., None] * b_re - co_im[..., None] * b_im
    bb_im = co_re[..., None] * b_im + co_im[..., None] * b_re
    c_re, c_im = c_re.astype(F32), c_im.astype(F32)
    tau = jnp.arange(t + 1, dtype=F32)[:, None, None, None]
    pmag = jnp.exp(tau * lr[None])
    pw_re, pw_im = pmag * jnp.cos(tau * li[None]), pmag * jnp.sin(tau * li[None])
    cp_re = c_re[None] * pw_re[:, :, :, None, :] - c_im[None] * pw_im[:, :, :, None, :]
    cp_im = c_re[None] * pw_im[:, :, :, None, :] + c_im[None] * pw_re[:, :, :, None, :]
    kern = jnp.einsum('tdghp,dgpk->tdghk', jnp.concatenate([cp_re, -cp_im], axis=-1),
                      jnp.concatenate([bb_re, bb_im], axis=-2), precision=hp)
    g, p, h = b_re.shape[1:]
    s_idx = jnp.arange(t)[:, None]
    t_idx = jnp.arange(t)[None, :]

    gb = S5_BLOCK
    nb = g // gb
    th, wide = t * h, t * gb * h
    col = jnp.arange(wide)
    row = jnp.arange(th)
    grp = jnp.arange(gb)[:, None, None]
    sel_th = ((row[None, :, None] // h == col[None, None, :] // (gb * h)) & (row[None, :, None] % h == col[None, None, :] % h)
              & ((col[None, None, :] // h) % gb == grp)).astype(BF16)
    colp = jnp.arange(gb * p)
    sel_p = ((jnp.arange(p)[None, :, None] == colp[None, None, :] % p) & (colp[None, None, :] // p == grp)).astype(BF16)

    def widen(m, sel):
        return jnp.einsum('qerc,ecy->qery', m.astype(BF16).astype(F32), sel.astype(F32)).astype(BF16)

    def interleave(m):
        return m.reshape(nb, gb, t, h, m.shape[-1]).transpose(0, 2, 1, 3, 4).reshape(nb, wide, m.shape[-1])

    def toeplitz(k_dir, lag):
        m = jnp.where((lag >= 0)[:, :, None, None, None], k_dir[jnp.clip(lag, 0, t)], 0.0)
        return m.transpose(2, 0, 4, 1, 3).reshape(nb, gb, th, th)

    mk = interleave(widen(toeplitz(kern[:, 0], t_idx - s_idx) + toeplitz(kern[:, 1], s_idx - t_idx), sel_th))

    def drive(d, exps):
        wr, wi = pw_re[exps, d], pw_im[exps, d]
        re = wr[..., None] * bb_re[d][None] - wi[..., None] * bb_im[d][None]
        im = wr[..., None] * bb_im[d][None] + wi[..., None] * bb_re[d][None]
        f = lambda m: interleave(widen(m.reshape(t, nb, gb, p, h).transpose(1, 2, 0, 4, 3).reshape(nb, gb, th, p), sel_p))
        return jnp.concatenate([f(re), f(im)], axis=-1)

    def read(d, exps):
        f = lambda m: widen(m.reshape(t, nb, gb, h, p).transpose(1, 2, 4, 0, 3).reshape(nb, gb, p, th),
                            sel_th).reshape(nb, gb * p, wide)
        return jnp.concatenate([f(cp_re[exps, d]), f(-cp_im[exps, d])], axis=1)

    ar = jnp.arange(t)
    ms_f, ms_r = drive(0, t - 1 - ar), drive(1, ar)
    mo_f, mo_r = read(0, ar + 1), read(1, t - ar)
    at = jnp.stack([pw_re[t, 0], pw_im[t, 0], pw_re[t, 1], pw_im[t, 1]], axis=1)
    at = at.reshape(nb, gb, 4, p).transpose(0, 2, 1, 3).reshape(nb, 4, gb * p)
    return mk, ms_f, ms_r, mo_f, mo_r, at


def _s5_kernel(uf_ref, mk_ref, msf_ref, msr_ref, mof_ref, mor_ref, a_ref, y_ref, sf_ref, sr_ref, hf_ref, hr_ref, *,
               batch, lat_tiles, ctx_tiles):
    w = a_ref.shape[-1]
    uf = uf_ref[...]
    sf_ref[...] = _dot(uf, msf_ref[...])
    sr_ref[...] = _dot(uf, msr_ref[...])
    a = a_ref[...]
    af = (a[0:1], a[1:2])
    ab = (a[2:3], a[3:4])
    row_id = lax.broadcasted_iota(jnp.int32, (SUBLANES, 2 * w), 0)

    def tile_step(s_ref, hp_ref, tile, state, coef, rows):
        off = pl.multiple_of(tile * SUBLANES, SUBLANES)
        s = s_ref[pl.ds(off, SUBLANES), :]
        hre, him = state
        are, aim = coef
        prev = jnp.zeros((SUBLANES, 2 * w), F32)
        for i in rows:
            row = jnp.concatenate([hre, him], axis=1)
            prev = jnp.where(row_id == i, jnp.broadcast_to(row, prev.shape), prev)
            sre, sim = s[i:i + 1, :w], s[i:i + 1, w:]
            hre, him = are * hre - aim * him + sre, are * him + aim * hre + sim
        hp_ref[pl.ds(off, SUBLANES), :] = prev
        return hre, him

    zero = jnp.zeros((1, w), F32)
    state = tuple((zero, zero) for _ in range(2 * batch))
    asc, desc = tuple(range(SUBLANES)), tuple(reversed(range(SUBLANES)))

    def phase(first_tile, n_tiles, state):
        def body(k, st):
            out = []
            for b in range(batch):
                out.append(tile_step(sf_ref, hf_ref, first_tile(b) + k, st[2 * b], af, asc))
                out.append(tile_step(sr_ref, hr_ref, first_tile(b) + n_tiles - 1 - k, st[2 * b + 1], ab, desc))
            return tuple(out)
        return lax.fori_loop(0, n_tiles, body, state)

    state = phase(lambda b: batch * lat_tiles + b * ctx_tiles, ctx_tiles, state)
    phase(lambda b: b * lat_tiles, lat_tiles, state)

    y_ref[...] = (_dot(uf, mk_ref[...]) + _dot(hf_ref[...].astype(BF16), mof_ref[...])
                  + _dot(hr_ref[...].astype(BF16), mor_ref[...]))


def _s5(uf, mats, i_even, lay):
    mk, ms_f, ms_r, mo_f, mo_r, at = mats
    nb, nc, wu = uf.shape
    ws = ms_f.shape[-1]
    t = S5_CHUNK
    full = lambda *shape: pl.BlockSpec((None,) + shape, lambda q: (q,) + (0,) * len(shape))
    mat = lambda *shape: pl.BlockSpec((None, None) + shape, lambda q: (i_even, q) + (0,) * len(shape))
    return pl.pallas_call(
        functools.partial(_s5_kernel, batch=lay.batch, lat_tiles=lay.seq // (t * SUBLANES),
                          ctx_tiles=lay.ctx // (t * SUBLANES)),
        grid=(nb,),
        in_specs=[full(nc, wu), mat(wu, wu), mat(wu, ws), mat(wu, ws), mat(ws, wu), mat(ws, wu),
                  mat(4, ws // 2)],
        out_specs=full(nc, wu),
        out_shape=jax.ShapeDtypeStruct((nb, nc, wu), F32),
        scratch_shapes=[pltpu.VMEM((nc, ws), F32)] * 4,
        compiler_params=_cparams(("parallel",)),
    )(uf, mk, ms_f, ms_r, mo_f, mo_r, at)


def _gelu_tanh(x):
    return 0.5 * x * (1.0 + jnp.tanh(math.sqrt(2.0 / math.pi) * (x + 0.044715 * (x * x * x))))


def _mixout_kernel(x_ref, mod_ref, g_ref, y_ref, u_ref, bg_ref, cg_ref, v_ref, cgp_ref, vp_ref, cgn_ref, vn_ref,
                   kp_ref, kn_ref, d_ref, wglu_ref, bglu_ref, cw_ref, cb_ref, wout_ref, o_ref, *, d, halo):
    _, _, gate = _mod3(mod_ref, 1, d)
    sw = u_ref.shape[1]
    tm = x_ref.shape[0]
    z = _gelu_tanh(y_ref[...] + d_ref[...] * u_ref[...].astype(F32))
    s5 = z * jax.nn.sigmoid(_dot(z.astype(BF16), wglu_ref[...]) + bglu_ref[...])
    gv = cg_ref[...].astype(F32) * v_ref[...].astype(F32)
    gv_before = cgp_ref[halo - 1:halo, :].astype(F32) * vp_ref[halo - 1:halo, :].astype(F32)
    gv_after = cgn_ref[0:1, :].astype(F32) * vn_ref[0:1, :].astype(F32)
    row = lax.broadcasted_iota(jnp.int32, (tm, 1), 0)
    prev = jnp.where(row == 0, gv_before, pltpu.roll(gv, 1, 0)) * kp_ref[...]
    nxt = jnp.where(row == tm - 1, gv_after, pltpu.roll(gv, tm - 1, 0)) * kn_ref[...]
    conv = cw_ref[0:1, :] * prev + cw_ref[1:2, :] * gv + cw_ref[2:3, :] * nxt + cb_ref[...]
    conv = bg_ref[...].astype(F32) * conv
    y = _dot(s5.astype(BF16), wout_ref[:sw, :]) + _dot(conv.astype(BF16), wout_ref[sw:, :])
    o_ref[...] = x_ref[...] + gate * _rms(y, g_ref[3:4, :])


def _mixout(xt, mod, norm_g, ys, p, keep_prev, keep_next, s5_d, w_glu, b_glu, conv_w, conv_b, w_out, layer, i_even,
            lay):
    r, d = xt.shape
    sw = ys.shape[1]
    tm = min(lay.tm, 256)
    halo = 16
    hb = tm // halo
    last_halo = r // halo - 1
    col = lambda c: pl.BlockSpec((tm, sw), lambda i: (i, c))
    before = lambda c: pl.BlockSpec((halo, sw), lambda i: (jnp.maximum(i * hb - 1, 0), c))
    after = lambda c: pl.BlockSpec((halo, sw), lambda i: (jnp.minimum((i + 1) * hb, last_halo), c))
    vec = lambda n: pl.BlockSpec((None, 1, n), lambda i: (i_even, 0, 0))
    lay_m = lay.with_tm(tm)
    return pl.pallas_call(
        functools.partial(_mixout_kernel, d=d, halo=halo),
        grid=(r // tm,),
        in_specs=[pl.BlockSpec((tm, d), lambda i: (i, 0)),
                  pl.BlockSpec((None, None, 1, N_MOD * d), lambda i: (layer, lay_m.mod_row(i), 0, 0)),
                  pl.BlockSpec((None, 6, d), lambda i: (layer, 0, 0)),
                  pl.BlockSpec((tm, sw), lambda i: (i, 0)),
                  col(0), col(1), col(2), col(3), before(2), before(3), after(2), after(3),
                  pl.BlockSpec((tm, 1), lambda i: (i, 0)),
                  pl.BlockSpec((tm, 1), lambda i: (i, 0)),
                  vec(sw),
                  pl.BlockSpec((None, sw, sw), lambda i: (i_even, 0, 0)),
                  vec(sw),
                  pl.BlockSpec((None, 3, sw), lambda i: (i_even, 0, 0)),
                  vec(sw),
                  pl.BlockSpec((None, 2 * sw, d), lambda i: (i_even, 0, 0))],
        out_specs=pl.BlockSpec((tm, d), lambda i: (i, 0)),
        out_shape=jax.ShapeDtypeStruct((r, d), F32),
        compiler_params=_cparams(("parallel",)),
    )(xt, mod, norm_g, ys, p, p, p, p, p, p, p, p, keep_prev, keep_next,
      s5_d.reshape(s5_d.shape[0], 1, sw), w_glu, b_glu.reshape(b_glu.shape[0], 1, sw), conv_w,
      conv_b.reshape(conv_b.shape[0], 1, sw), w_out)


def _attn_kernel(lam_ref, q_ref, kc_ref, vct_ref, *rest, latents, tk, out_scale):
    if latents:
        k_ref, vt_ref = rest[:2]
        rest = rest[2:]
    sub_ref, o_ref, q2_ref, s_ref, m_ref, al_ref, l_ref, acc_ref = rest
    tq, hd = q_ref.shape
    qt = q_ref[...].astype(F32).T
    row = lax.broadcasted_iota(jnp.int32, (hd, tq), 0)
    q2_ref[:, :tq] = jnp.where(row < hd // 2, qt, 0.0).astype(BF16)
    q2_ref[:, tq:] = jnp.where(row >= hd // 2, qt, 0.0).astype(BF16)

    def scores(c):
        return _dot(k_ref[c * tk:(c + 1) * tk, :], q2_ref[...])

    if latents:
        s_ref[0] = scores(0)

    s = _dot(kc_ref[...], q2_ref[...])
    m0 = jnp.max(s, axis=0, keepdims=True)
    pr = jnp.exp2(s - m0)
    m_ref[...] = m0
    l_ref[...] = jnp.sum(pr, axis=0, keepdims=True)
    acc_ref[...] = _dot(vct_ref[...], pr.astype(BF16))

    def fold_max(buf):
        m_prev = m_ref[...]
        m_new = jnp.maximum(m_prev, jnp.max(s_ref[buf], axis=0, keepdims=True))
        al_ref[...] = jnp.exp2(m_prev - m_new)
        m_ref[...] = m_new

    def latent_chunks():
        n_chunks = k_ref.shape[0] // tk
        fold_max(0)
        for c in range(n_chunks):
            cur, nxt, more = c % 2, (c + 1) % 2, c + 1 < n_chunks
            if more:
                s_ref[nxt] = scores(c + 1)
            alpha = al_ref[...]
            pr = jnp.exp2(s_ref[cur] - m_ref[...])
            l_ref[...] = alpha * l_ref[...] + jnp.sum(pr, axis=0, keepdims=True)
            pv = _dot(vt_ref[:, c * tk:(c + 1) * tk], pr.astype(BF16))
            if more:
                fold_max(nxt)
            acc_ref[...] = alpha * acc_ref[...] + pv

    if latents:
        pl.when(pl.program_id(2) >= 0)(latent_chunks)

    o = acc_ref[...] * (1.0 / l_ref[...])
    o = o[:, :tq] - lam_ref[0] * o[:, tq:]
    o = o * lax.rsqrt(jnp.mean(o * o, axis=0, keepdims=True) + EPS)
    o = o * jnp.tile(sub_ref[...], (1, tq // LANES)) * out_scale
    o_ref[...] = o.T.astype(o_ref.dtype)


def _attn(qkv, lam, subln, lam_init, lay, need_ctx):
    hd = subln.shape[-1]
    heads = qkv.shape[1] // (3 * hd)
    d = heads * hd
    vt = qkv[:, 2 * d:].T
    ctx_block = lay.batch * lay.seq // lay.ctx
    sub = jnp.broadcast_to(subln.reshape(hd, 1), (hd, LANES))
    smem = pl.BlockSpec(memory_space=pltpu.SMEM)
    ctx_specs = [pl.BlockSpec((lay.ctx, hd), lambda b, h, qi: (ctx_block + b, heads + h)),
                 pl.BlockSpec((hd, lay.ctx), lambda b, h, qi: (h, ctx_block + b))]
    sub_spec = pl.BlockSpec((hd, LANES), lambda b, h, qi: (0, 0))

    def scratch(tq, tk):
        return ([pltpu.VMEM((hd, 2 * tq), BF16), pltpu.VMEM((2, tk, 2 * tq), F32)]
                + [pltpu.VMEM((1, 2 * tq), F32)] * 3 + [pltpu.VMEM((hd, 2 * tq), F32)])

    tq = _pick(lay.seq, (256, 128))
    tk = _pick(lay.seq, (512, 256, 128))
    n_q = lay.seq // tq
    a_lat = pl.pallas_call(
        functools.partial(_attn_kernel, latents=True, tk=tk, out_scale=1.0 - lam_init),
        grid=(lay.batch, heads, n_q),
        in_specs=[smem, pl.BlockSpec((tq, hd), lambda b, h, qi: (b * n_q + qi, h))] + ctx_specs
        + [pl.BlockSpec((lay.seq, hd), lambda b, h, qi: (b, heads + h)),
           pl.BlockSpec((hd, lay.seq), lambda b, h, qi: (h, b)), sub_spec],
        out_specs=pl.BlockSpec((tq, hd), lambda b, h, qi: (b * n_q + qi, h)),
        out_shape=jax.ShapeDtypeStruct((lay.batch * lay.seq, d), BF16),
        scratch_shapes=scratch(tq, tk),
        compiler_params=_cparams(("parallel", "parallel", "arbitrary")),
    )(lam, qkv, qkv, vt, qkv, vt, sub)
    if not need_ctx:
        return a_lat, None
    a_ctx = pl.pallas_call(
        functools.partial(_attn_kernel, latents=False, tk=SUBLANES, out_scale=1.0 - lam_init),
        grid=(lay.batch, heads, 1),
        in_specs=[smem, pl.BlockSpec((lay.ctx, hd), lambda b, h, qi: (ctx_block + b, h))] + ctx_specs + [sub_spec],
        out_specs=pl.BlockSpec((lay.ctx, hd), lambda b, h, qi: (b, h)),
        out_shape=jax.ShapeDtypeStruct((lay.batch * lay.ctx, d), BF16),
        scratch_shapes=scratch(lay.ctx, SUBLANES),
        compiler_params=_cparams(("parallel", "parallel", "arbitrary")),
    )(lam, qkv, qkv, vt, sub)
    return a_lat, a_ctx


def _attnout_kernel(x_ref, mod_ref, g_ref, a_ref, *rest, d, n_lat_tiles):
    _, _, gate = _mod3(mod_ref, 1, d)
    if len(rest) == 3:
        actx_ref, w_ref, o_ref = rest
        a = jnp.where(pl.program_id(0) < n_lat_tiles, a_ref[...], actx_ref[...])
    else:
        w_ref, o_ref = rest
        a = a_ref[...]
    y = _dot(a, w_ref[...])
    o_ref[...] = x_ref[...] + gate * _rms(y, g_ref[3:4, :])


def _attnout(xt, mod, norm_g, a_lat, a_ctx, w_o, layer, i_odd, lay):
    d = xt.shape[1]
    tm = min(lay.tm, 256)
    lay_m = lay.with_tm(tm)
    n_lat = lay_m.n_lat_tiles
    r = a_lat.shape[0] + (0 if a_ctx is None else a_ctx.shape[0])
    a_specs = [pl.BlockSpec((tm, d), lambda i: (jnp.minimum(i, n_lat - 1), 0))]
    a_args = [a_lat]
    if a_ctx is not None:
        a_specs.append(pl.BlockSpec((tm, d), lambda i: (jnp.maximum(i - n_lat, 0), 0)))
        a_args.append(a_ctx)
    return pl.pallas_call(
        functools.partial(_attnout_kernel, d=d, n_lat_tiles=n_lat),
        grid=(r // tm,),
        in_specs=[pl.BlockSpec((tm, d), lambda i: (i, 0)),
                  pl.BlockSpec((None, None, 1, N_MOD * d), lambda i: (layer, lay_m.mod_row(i), 0, 0)),
                  pl.BlockSpec((None, 6, d), lambda i: (layer, 0, 0))] + a_specs
        + [pl.BlockSpec((None, d, d), lambda i: (i_odd, 0, 0))],
        out_specs=pl.BlockSpec((tm, d), lambda i: (i, 0)),
        out_shape=jax.ShapeDtypeStruct((r, d), F32),
        compiler_params=_cparams(("parallel",)),
    )(xt, mod, norm_g, *a_args, w_o)


class _Layout:
    def __init__(self, batch, seq, ctx, tm, q_scale):
        self.batch, self.seq, self.ctx, self.tm, self.q_scale = batch, seq, ctx, tm, q_scale
        self.tiles_per_seq = seq // tm
        self.n_lat_tiles = batch * self.tiles_per_seq

    def with_tm(self, tm):
        return _Layout(self.batch, self.seq, self.ctx, tm, self.q_scale)

    def mod_row(self, i):
        return jnp.minimum(i // self.tiles_per_seq, self.batch)


def kernel(x, c, ctx, c_ctx, w_mod, b_mod, norm_g, ffn_wg, ffn_wu, ffn_wd, mix_w_in, mix_w_out, s5_lam_re, s5_lam_im, s5_log_step, s5_b_re, s5_b_im, s5_c_re, s5_c_im, s5_d, s5_w_glu, s5_b_glu, conv_w, conv_b, attn_w_qkv, attn_w_o, attn_lambda, attn_subln):
    batch, seq, d = x.shape
    n_ctx = ctx.shape[1]
    depth = w_mod.shape[0]
    hd = attn_subln.shape[-1]
    dh = attn_lambda.shape[-1]
    g, p, h = s5_b_re.shape[2:]
    sw = g * h
    t = S5_CHUNK
    assert batch + 1 <= SUBLANES and conv_w.shape[-1] == sw and mix_w_in.shape[-1] == 4 * sw
    assert g % S5_BLOCK == 0 and (S5_BLOCK * p) % LANES == 0 and (S5_BLOCK * h * t) % LANES == 0
    assert seq % (t * SUBLANES) == 0 and n_ctx % (t * SUBLANES) == 0 and seq % n_ctx == 0 and hd == 2 * dh
    tm = _pick(math.gcd(seq, batch * n_ctx), (512, 256, 128, 64, 32, 16))
    lay = _Layout(batch, seq, n_ctx, tm, dh ** -0.5 * math.log2(math.e))
    r = batch * (seq + n_ctx)

    xt = jnp.concatenate([x.reshape(batch * seq, d), ctx.reshape(batch * n_ctx, d)], axis=0)
    cc = jnp.zeros((SUBLANES, d), F32).at[:batch].set(c).at[batch].set(c_ctx)
    mod = _ada(cc, w_mod, b_mod).reshape(depth, SUBLANES, 1, N_MOD * d)

    wg, wu, wd = ffn_wg.astype(BF16), ffn_wu.astype(BF16), ffn_wd.astype(BF16)
    w_in, w_out, w_glu = mix_w_in.astype(BF16), mix_w_out.astype(BF16), s5_w_glu.astype(BF16)
    w_qkv, w_o = attn_w_qkv.astype(BF16), attn_w_o.astype(BF16)

    cos, sin = _rope_tables(seq)
    pos = jnp.concatenate([jnp.tile(jnp.arange(seq), batch), jnp.tile(jnp.arange(n_ctx), batch)])
    last = jnp.concatenate([jnp.full((batch * seq,), seq - 1), jnp.full((batch * n_ctx,), n_ctx - 1)])
    keep_prev = (pos != 0).astype(F32)[:, None]
    keep_next = (pos != last).astype(F32)[:, None]
    nc = r // t
    mats = jax.vmap(functools.partial(_s5_matrices, t=t))(s5_lam_re, s5_lam_im, s5_log_step, s5_b_re, s5_b_im,
                                                          s5_c_re, s5_c_im)

    for layer in range(depth):
        i = layer // 2
        xt = _ffn(xt, mod, norm_g, wg, wu, wd, layer, 0, 0, lay)
        if layer % 2 == 0:
            pj = _proj(xt, mod, norm_g, w_in, i, layer, lay)
            nb, wb = g // S5_BLOCK, S5_BLOCK * h
            uf = pj[:, :sw].reshape(nc, t, nb, wb).transpose(2, 0, 1, 3).reshape(nb, nc, t * wb)
            ys = _s5(uf, mats, i, lay)
            ys = ys.reshape(nb, nc, t, wb).transpose(1, 2, 0, 3).reshape(r, sw)
            xt = _mixout(xt, mod, norm_g, ys, pj, keep_prev, keep_next, s5_d, w_glu, s5_b_glu, conv_w, conv_b, w_out,
                         layer, i, lay)
        else:
            lam_init = 0.8 - 0.6 * math.exp(-0.3 * layer)
            lv = attn_lambda[i].astype(F32)
            lam = (jnp.exp(jnp.sum(lv[0] * lv[1])) - jnp.exp(jnp.sum(lv[2] * lv[3])) + lam_init).reshape(1)
            qkv = _proj(xt, mod, norm_g, w_qkv, i, layer, lay, rope=(cos, sin, 2 * d, d))
            a_lat, a_ctx = _attn(qkv, lam, attn_subln[i], lam_init, lay, need_ctx=layer < depth - 1)
            xt = _attnout(xt, mod, norm_g, a_lat, a_ctx, w_o, layer, i, lay)
        xt = _ffn(xt, mod, norm_g, wg, wu, wd, layer, 2, 1, lay, rows=batch * seq if layer == depth - 1 else None)
    return xt.reshape(batch, seq, d)
```

```python
import functools
import math

import jax
import jax.numpy as jnp
from jax import lax
from jax.experimental import pallas as pl
from jax.experimental.pallas import tpu as pltpu

F32 = jnp.float32
BF16 = jnp.bfloat16
EPS = 1e-6
N_MOD = 9
GRID_W = 64
ROPE_BASE = 10000.0
S5_CHUNK = 16
S5_BLOCK = 4
SUBLANES = 8
LANES = 128
ROW_CHUNK = 16
ROW_UNROLL = 8
VMEM_LIMIT = 56 * 1024 * 1024


def _cparams(sem):
    return pltpu.CompilerParams(dimension_semantics=sem, vmem_limit_bytes=VMEM_LIMIT)


def _dot(a, b):
    return jnp.dot(a, b, preferred_element_type=F32)


def _rms(x, g):
    return x * lax.rsqrt(jnp.mean(x * x, axis=-1, keepdims=True) + EPS) * g


def _for_row_chunks(n_rows, fn):
    def body(i, carry):
        fn(pl.ds(pl.multiple_of(i * ROW_CHUNK, ROW_CHUNK), ROW_CHUNK))
        return carry
    lax.fori_loop(0, n_rows // ROW_CHUNK, body, 0, unroll=ROW_UNROLL)


def _norm_modulate(x_ref, h_ref, g, scale, shift):
    gs = g * (1.0 + scale)

    def chunk(rows):
        x = x_ref[rows, :]
        h_ref[rows, :] = (_rms(x, gs) + shift).astype(h_ref.dtype)
    _for_row_chunks(x_ref.shape[0], chunk)


def _gated_residual(x_ref, y_ref, o_ref, gg):
    def chunk(rows):
        o_ref[rows, :] = x_ref[rows, :] + _rms(y_ref[rows, :], gg)
    _for_row_chunks(x_ref.shape[0], chunk)


def _pick(n, candidates):
    for c in candidates:
        if n % c == 0:
            return c
    return n


def _ada_kernel(c_ref, w_ref, b_ref, o_ref):
    s = c_ref[...]
    s = s * jax.nn.sigmoid(s)
    o_ref[...] = _dot(s.astype(BF16), w_ref[...].astype(BF16)) + b_ref[...]


def _ada(cc, w_mod, b_mod):
    depth, d, n = w_mod.shape
    tn = _pick(n, (1024, 512, 256, 128))
    return pl.pallas_call(
        _ada_kernel,
        grid=(depth, n // tn),
        in_specs=[pl.BlockSpec((SUBLANES, d), lambda l, j: (0, 0)),
                  pl.BlockSpec((None, d, tn), lambda l, j: (l, 0, j)),
                  pl.BlockSpec((None, 1, tn), lambda l, j: (l, 0, j))],
        out_specs=pl.BlockSpec((None, SUBLANES, tn), lambda l, j: (l, 0, j)),
        out_shape=jax.ShapeDtypeStruct((depth, SUBLANES, n), F32),
        compiler_params=_cparams(("arbitrary", "arbitrary")),
    )(cc, w_mod, b_mod.reshape(depth, 1, n))


def _mod3(mod_ref, sub, d):
    return tuple(mod_ref[:, (3 * sub + k) * d:(3 * sub + k + 1) * d] for k in range(3))


def _ffn_kernel(x_ref, mod_ref, g_ref, wg_ref, wu_ref, wd_ref, o_ref, h_ref, acc_ref, *, sub, d):
    j = pl.program_id(1)
    shift, scale, gate = _mod3(mod_ref, sub, d)

    @pl.when(j == 0)
    def _():
        _norm_modulate(x_ref, h_ref, g_ref[2 * sub:2 * sub + 1, :], scale, shift)
        acc_ref[...] = jnp.zeros_like(acc_ref)

    h = h_ref[...]
    a = _dot(h, wg_ref[...])
    u = _dot(h, wu_ref[...])
    act = (a * jax.nn.sigmoid(a) * u).astype(BF16)
    acc_ref[...] += _dot(act, wd_ref[...])

    @pl.when(j == pl.num_programs(1) - 1)
    def _():
        _gated_residual(x_ref, acc_ref, o_ref, 0.5 * gate * g_ref[2 * sub + 1:2 * sub + 2, :])


def _ffn(xt, mod, norm_g, wg, wu, wd, layer, sub, widx, lay, rows=None):
    d = xt.shape[1]
    r = xt.shape[0] if rows is None else rows
    f = wg.shape[-1]
    tm, tf = lay.tm, _pick(f, (512, 256, 128))
    return pl.pallas_call(
        functools.partial(_ffn_kernel, sub=sub, d=d),
        grid=(r // tm, f // tf),
        in_specs=[pl.BlockSpec((tm, d), lambda i, j: (i, 0)),
                  pl.BlockSpec((None, None, 1, N_MOD * d), lambda i, j: (layer, lay.mod_row(i), 0, 0)),
                  pl.BlockSpec((None, 6, d), lambda i, j: (layer, 0, 0)),
                  pl.BlockSpec((None, None, d, tf), lambda i, j: (layer, widx, 0, j)),
                  pl.BlockSpec((None, None, d, tf), lambda i, j: (layer, widx, 0, j)),
                  pl.BlockSpec((None, None, tf, d), lambda i, j: (layer, widx, j, 0))],
        out_specs=pl.BlockSpec((tm, d), lambda i, j: (i, 0)),
        out_shape=jax.ShapeDtypeStruct((r, d), F32),
        scratch_shapes=[pltpu.VMEM((tm, d), BF16), pltpu.VMEM((tm, d), F32)],
        compiler_params=_cparams(("parallel", "arbitrary")),
    )(xt, mod, norm_g, wg, wu, wd)


def _proj_kernel(x_ref, mod_ref, g_ref, w_ref, *rest, d, tn, n_q, n_rot):
    shift, scale, _ = _mod3(mod_ref, 1, d)
    o_ref, h_ref = rest[-2:]
    _norm_modulate(x_ref, h_ref, g_ref[2:3, :], scale, shift)
    h = h_ref[...]
    lane = lax.broadcasted_iota(jnp.int32, (1, LANES), 1)
    first_half = (lane % 64) < 32
    for c in range(w_ref.shape[1] // tn):
        col = c * tn
        y = _dot(h, w_ref[:, col:col + tn])
        if col >= n_rot:
            o_ref[:, col:col + tn] = y.astype(o_ref.dtype)
            continue
        cos_ref, sin_ref = rest[:2] if col < n_q else rest[2:4]
        for cc in range(tn // LANES):
            yc = y[:, cc * LANES:(cc + 1) * LANES]
            partner = jnp.where(first_half, pltpu.roll(yc, LANES - 32, 1), pltpu.roll(yc, 32, 1))
            o_ref[:, col + cc * LANES:col + (cc + 1) * LANES] = (
                yc * cos_ref[...] + partner * sin_ref[...]).astype(o_ref.dtype)


def _proj(xt, mod, norm_g, w, widx, layer, lay, rope=None):
    r, d = xt.shape
    n = w.shape[-1]
    tm = lay.tm
    in_specs = [pl.BlockSpec((tm, d), lambda i: (i, 0)),
                pl.BlockSpec((None, None, 1, N_MOD * d), lambda i: (layer, lay.mod_row(i), 0, 0)),
                pl.BlockSpec((None, 6, d), lambda i: (layer, 0, 0)),
                pl.BlockSpec((None, d, n), lambda i: (widx, 0, 0), pipeline_mode=pl.Buffered(1))]
    if rope is None:
        tn = _pick(n, (1024, 512, 256, 128))
        n_rot = n_q = 0
        tables = ()
    else:
        cos, sin, n_rot, n_q = rope
        tn = _pick(math.gcd(n_q, n), (1024, 512, 256, 128))
        tps = lay.tiles_per_seq
        one, zero = jnp.ones((tm, LANES), F32), jnp.zeros((tm, LANES), F32)
        cos_tab = jnp.concatenate([cos * lay.q_scale, cos, one * lay.q_scale, one])
        sin_tab = jnp.concatenate([sin * lay.q_scale, sin, zero, zero])
        q_map = lambda i: (jnp.where(i < lay.n_lat_tiles, i % tps, 2 * tps), 0)
        k_map = lambda i: (jnp.where(i < lay.n_lat_tiles, tps + i % tps, 2 * tps + 1), 0)
        tables = (cos_tab, sin_tab, cos_tab, sin_tab)
        in_specs += [pl.BlockSpec((tm, LANES), m) for m in (q_map, q_map, k_map, k_map)]
    return pl.pallas_call(
        functools.partial(_proj_kernel, d=d, tn=tn, n_q=n_q, n_rot=n_rot),
        grid=(r // tm,),
        in_specs=in_specs,
        out_specs=pl.BlockSpec((tm, n), lambda i: (i, 0)),
        out_shape=jax.ShapeDtypeStruct((r, n), BF16),
        scratch_shapes=[pltpu.VMEM((tm, d), BF16)],
        compiler_params=_cparams(("parallel",)),
    )(xt, mod, norm_g, w, *tables)


def _rope_tables(n):
    rows = n // GRID_W
    row = jnp.repeat(jnp.arange(rows), GRID_W).astype(F32)
    col = jnp.tile(jnp.arange(GRID_W), rows).astype(F32)
    pairs = 16
    freqs = jnp.power(ROPE_BASE, -jnp.arange(pairs, dtype=F32) / pairs)
    ang = jnp.concatenate([row[:, None] * freqs, col[:, None] * freqs], axis=-1)
    cos, sin = jnp.cos(ang), jnp.sin(ang)
    return jnp.tile(cos, (1, 4)), jnp.concatenate([-sin, sin, -sin, sin], axis=-1)


def _s5_matrices(lam_re, lam_im, log_step, b_re, b_im, c_re, c_im, t):
    hp = lax.Precision.HIGHEST
    lam_re, lam_im = lam_re.astype(F32), lam_im.astype(F32)
    dt = jnp.exp(log_step.astype(F32))[..., None]
    lr, li = lam_re * dt, lam_im * dt
    mag = jnp.exp(lr)
    a_re, a_im = mag * jnp.cos(li), mag * jnp.sin(li)
    inv = 1.0 / (lam_re * lam_re + lam_im * lam_im)
    co_re = ((a_re - 1.0) * lam_re + a_im * lam_im) * inv
    co_im = (a_im * lam_re - (a_re - 1.0) * lam_im) * inv
    b_re, b_im = b_re.astype(F32), b_im.astype(F32)
    bb_re = co_re[..., None] * b_re - co_im[..., None] * b_im
    bb_im = co_re[..., None] * b_im + co_im[..., None] * b_re
    c_re, c_im = c_re.astype(F32), c_im.astype(F32)
    tau = jnp.arange(t + 1, dtype=F32)[:, None, None, None]
    pmag = jnp.exp(tau * lr[None])
    pw_re, pw_im = pmag * jnp.cos(tau * li[None]), pmag * jnp.sin(tau * li[None])
    cp_re = c_re[None] * pw_re[:, :, :, None, :] - c_im[None] * pw_im[:, :, :, None, :]
    cp_im = c_re[None] * pw_im[:, :, :, None, :] + c_im[None] * pw_re[:, :, :, None, :]
    kern = jnp.einsum('tdghp,dgpk->tdghk', jnp.concatenate([cp_re, -cp_im], axis=-1),
                      jnp.concatenate([bb_re, bb_im], axis=-2), precision=hp)
    g, p, h = b_re.shape[1:]
    s_idx = jnp.arange(t)[:, None]
    t_idx = jnp.arange(t)[None, :]

    gb = S5_BLOCK
    nb = g // gb
    th, wide = t * h, t * gb * h
    col = jnp.arange(wide)
    row = jnp.arange(th)
    grp = jnp.arange(gb)[:, None, None]
    sel_th = ((row[None, :, None] // h == col[None, None, :] // (gb * h)) & (row[None, :, None] % h == col[None, None, :] % h)
              & ((col[None, None, :] // h) % gb == grp)).astype(BF16)
    colp = jnp.arange(gb * p)
    sel_p = ((jnp.arange(p)[None, :, None] == colp[None, None, :] % p) & (colp[None, None, :] // p == grp)).astype(BF16)

    def widen(m, sel):
        return jnp.einsum('qerc,ecy->qery', m.astype(BF16).astype(F32), sel.astype(F32)).astype(BF16)

    def interleave(m):
        return m.reshape(nb, gb, t, h, m.shape[-1]).transpose(0, 2, 1, 3, 4).reshape(nb, wide, m.shape[-1])

    def toeplitz(k_dir, lag):
        m = jnp.where((lag >= 0)[:, :, None, None, None], k_dir[jnp.clip(lag, 0, t)], 0.0)
        return m.transpose(2, 0, 4, 1, 3).reshape(nb, gb, th, th)

    mk = interleave(widen(toeplitz(kern[:, 0], t_idx - s_idx) + toeplitz(kern[:, 1], s_idx - t_idx), sel_th))

    def drive(d, exps):
        wr, wi = pw_re[exps, d], pw_im[exps, d]
        re = wr[..., None] * bb_re[d][None] - wi[..., None] * bb_im[d][None]
        im = wr[..., None] * bb_im[d][None] + wi[..., None] * bb_re[d][None]
        f = lambda m: interleave(widen(m.reshape(t, nb, gb, p, h).transpose(1, 2, 0, 4, 3).reshape(nb, gb, th, p), sel_p))
        return jnp.concatenate([f(re), f(im)], axis=-1)

    def read(d, exps):
        f = lambda m: widen(m.reshape(t, nb, gb, h, p).transpose(1, 2, 4, 0, 3).reshape(nb, gb, p, th),
                            sel_th).reshape(nb, gb * p, wide)
        return jnp.concatenate([f(cp_re[exps, d]), f(-cp_im[exps, d])], axis=1)

    ar = jnp.arange(t)
    ms_f, ms_r = drive(0, t - 1 - ar), drive(1, ar)
    mo_f, mo_r = read(0, ar + 1), read(1, t - ar)
    at = jnp.stack([pw_re[t, 0], pw_im[t, 0], pw_re[t, 1], pw_im[t, 1]], axis=1)
    at = at.reshape(nb, gb, 4, p).transpose(0, 2, 1, 3).reshape(nb, 4, gb * p)
    return mk, ms_f, ms_r, mo_f, mo_r, at


def _s5_kernel(uf_ref, mk_ref, msf_ref, msr_ref, mof_ref, mor_ref, a_ref, y_ref, sf_ref, sr_ref, hf_ref, hr_ref, *,
               batch, lat_tiles, ctx_tiles):
    w = a_ref.shape[-1]
    uf = uf_ref[...]
    sf_ref[...] = _dot(uf, msf_ref[...])
    sr_ref[...] = _dot(uf, msr_ref[...])
    a = a_ref[...]
    af = (a[0:1], a[1:2])
    ab = (a[2:3], a[3:4])
    row_id = lax.broadcasted_iota(jnp.int32, (SUBLANES, 2 * w), 0)

    def tile_step(s_ref, hp_ref, tile, state, coef, rows):
        off = pl.multiple_of(tile * SUBLANES, SUBLANES)
        s = s_ref[pl.ds(off, SUBLANES), :]
        hre, him = state
        are, aim = coef
        prev = jnp.zeros((SUBLANES, 2 * w), F32)
        for i in rows:
            row = jnp.concatenate([hre, him], axis=1)
            prev = jnp.where(row_id == i, jnp.broadcast_to(row, prev.shape), prev)
            sre, sim = s[i:i + 1, :w], s[i:i + 1, w:]
            hre, him = are * hre - aim * him + sre, are * him + aim * hre + sim
        hp_ref[pl.ds(off, SUBLANES), :] = prev
        return hre, him

    zero = jnp.zeros((1, w), F32)
    state = tuple((zero, zero) for _ in range(2 * batch))
    asc, desc = tuple(range(SUBLANES)), tuple(reversed(range(SUBLANES)))

    def phase(first_tile, n_tiles, state):
        def body(k, st):
            out = []
            for b in range(batch):
                out.append(tile_step(sf_ref, hf_ref, first_tile(b) + k, st[2 * b], af, asc))
                out.append(tile_step(sr_ref, hr_ref, first_tile(b) + n_tiles - 1 - k, st[2 * b + 1], ab, desc))
            return tuple(out)
        return lax.fori_loop(0, n_tiles, body, state)

    state = phase(lambda b: batch * lat_tiles + b * ctx_tiles, ctx_tiles, state)
    phase(lambda b: b * lat_tiles, lat_tiles, state)

    y_ref[...] = (_dot(uf, mk_ref[...]) + _dot(hf_ref[...].astype(BF16), mof_ref[...])
                  + _dot(hr_ref[...].astype(BF16), mor_ref[...]))


def _s5(uf, mats, i_even, lay):
    mk, ms_f, ms_r, mo_f, mo_r, at = mats
    nb, nc, wu = uf.shape
    ws = ms_f.shape[-1]
    t = S5_CHUNK
    full = lambda *shape: pl.BlockSpec((None,) + shape, lambda q: (q,) + (0,) * len(shape))
    mat = lambda *shape: pl.BlockSpec((None, None) + shape, lambda q: (i_even, q) + (0,) * len(shape))
    return pl.pallas_call(
        functools.partial(_s5_kernel, batch=lay.batch, lat_tiles=lay.seq // (t * SUBLANES),
                          ctx_tiles=lay.ctx // (t * SUBLANES)),
        grid=(nb,),
        in_specs=[full(nc, wu), mat(wu, wu), mat(wu, ws), mat(wu, ws), mat(ws, wu), mat(ws, wu),
                  mat(4, ws // 2)],
        out_specs=full(nc, wu),
        out_shape=jax.ShapeDtypeStruct((nb, nc, wu), F32),
        scratch_shapes=[pltpu.VMEM((nc, ws), F32)] * 4,
        compiler_params=_cparams(("parallel",)),
    )(uf, mk, ms_f, ms_r, mo_f, mo_r, at)


def _gelu_tanh(x):
    return 0.5 * x * (1.0 + jnp.tanh(math.sqrt(2.0 / math.pi) * (x + 0.044715 * (x * x * x))))


def _mixout_kernel(x_ref, mod_ref, g_ref, y_ref, u_ref, bg_ref, cg_ref, v_ref, cgp_ref, vp_ref, cgn_ref, vn_ref,
                   kp_ref, kn_ref, d_ref, wglu_ref, bglu_ref, cw_ref, cb_ref, wout_ref, o_ref, *, d, halo):
    _, _, gate = _mod3(mod_ref, 1, d)
    sw = u_ref.shape[1]
    tm = x_ref.shape[0]
    z = _gelu_tanh(y_ref[...] + d_ref[...] * u_ref[...].astype(F32))
    s5 = z * jax.nn.sigmoid(_dot(z.astype(BF16), wglu_ref[...]) + bglu_ref[...])
    gv = cg_ref[...].astype(F32) * v_ref[...].astype(F32)
    gv_before = cgp_ref[halo - 1:halo, :].astype(F32) * vp_ref[halo - 1:halo, :].astype(F32)
    gv_after = cgn_ref[0:1, :].astype(F32) * vn_ref[0:1, :].astype(F32)
    row = lax.broadcasted_iota(jnp.int32, (tm, 1), 0)
    prev = jnp.where(row == 0, gv_before, pltpu.roll(gv, 1, 0)) * kp_ref[...]
    nxt = jnp.where(row == tm - 1, gv_after, pltpu.roll(gv, tm - 1, 0)) * kn_ref[...]
    conv = cw_ref[0:1, :] * prev + cw_ref[1:2, :] * gv + cw_ref[2:3, :] * nxt + cb_ref[...]
    conv = bg_ref[...].astype(F32) * conv
    y = _dot(s5.astype(BF16), wout_ref[:sw, :]) + _dot(conv.astype(BF16), wout_ref[sw:, :])
    o_ref[...] = x_ref[...] + gate * _rms(y, g_ref[3:4, :])


def _mixout(xt, mod, norm_g, ys, p, keep_prev, keep_next, s5_d, w_glu, b_glu, conv_w, conv_b, w_out, layer, i_even,
            lay):
    r, d = xt.shape
    sw = ys.shape[1]
    tm = min(lay.tm, 256)
    halo = 16
    hb = tm // halo
    last_halo = r // halo - 1
    col = lambda c: pl.BlockSpec((tm, sw), lambda i: (i, c))
    before = lambda c: pl.BlockSpec((halo, sw), lambda i: (jnp.maximum(i * hb - 1, 0), c))
    after = lambda c: pl.BlockSpec((halo, sw), lambda i: (jnp.minimum((i + 1) * hb, last_halo), c))
    vec = lambda n: pl.BlockSpec((None, 1, n), lambda i: (i_even, 0, 0))
    lay_m = lay.with_tm(tm)
    return pl.pallas_call(
        functools.partial(_mixout_kernel, d=d, halo=halo),
        grid=(r // tm,),
        in_specs=[pl.BlockSpec((tm, d), lambda i: (i, 0)),
                  pl.BlockSpec((None, None, 1, N_MOD * d), lambda i: (layer, lay_m.mod_row(i), 0, 0)),
                  pl.BlockSpec((None, 6, d), lambda i: (layer, 0, 0)),
                  pl.BlockSpec((tm, sw), lambda i: (i, 0)),
                  col(0), col(1), col(2), col(3), before(2), before(3), after(2), after(3),
                  pl.BlockSpec((tm, 1), lambda i: (i, 0)),
                  pl.BlockSpec((tm, 1), lambda i: (i, 0)),
                  vec(sw),
                  pl.BlockSpec((None, sw, sw), lambda i: (i_even, 0, 0)),
                  vec(sw),
                  pl.BlockSpec((None, 3, sw), lambda i: (i_even, 0, 0)),
                  vec(sw),
                  pl.BlockSpec((None, 2 * sw, d), lambda i: (i_even, 0, 0))],
        out_specs=pl.BlockSpec((tm, d), lambda i: (i, 0)),
        out_shape=jax.ShapeDtypeStruct((r, d), F32),
        compiler_params=_cparams(("parallel",)),
    )(xt, mod, norm_g, ys, p, p, p, p, p, p, p, p, keep_prev, keep_next,
      s5_d.reshape(s5_d.shape[0], 1, sw), w_glu, b_glu.reshape(b_glu.shape[0], 1, sw), conv_w,
      conv_b.reshape(conv_b.shape[0], 1, sw), w_out)


def _attn_kernel(lam_ref, q_ref, kc_ref, vct_ref, *rest, latents, tk, q_tiles, out_scale):
    if latents:
        k_ref, vt_ref = rest[:2]
        rest = rest[2:]
    sub_ref, o_ref, q2_ref, s_ref, m_ref, al_ref, l_ref, acc_ref = rest
    hd = q_ref.shape[1]
    tq = q_ref.shape[0] // q_tiles
    for t in range(q_tiles):
        rows = slice(t * tq, (t + 1) * tq)
        qt = q_ref[rows, :].astype(F32).T
        row = lax.broadcasted_iota(jnp.int32, (hd, tq), 0)
        q2_ref[:, :tq] = jnp.where(row < hd // 2, qt, 0.0).astype(BF16)
        q2_ref[:, tq:] = jnp.where(row >= hd // 2, qt, 0.0).astype(BF16)

        def scores(c):
            return _dot(k_ref[c * tk:(c + 1) * tk, :], q2_ref[...])

        if latents:
            s_ref[0] = scores(0)

        s = _dot(kc_ref[...], q2_ref[...])
        m0 = jnp.max(s, axis=0, keepdims=True)
        pr = jnp.exp2(s - m0)
        m_ref[...] = m0
        l_ref[...] = jnp.sum(pr, axis=0, keepdims=True)
        acc_ref[...] = _dot(vct_ref[...], pr.astype(BF16))

        def fold_max(buf):
            m_prev = m_ref[...]
            m_new = jnp.maximum(m_prev, jnp.max(s_ref[buf], axis=0, keepdims=True))
            al_ref[...] = jnp.exp2(m_prev - m_new)
            m_ref[...] = m_new

        def latent_chunks():
            n_chunks = k_ref.shape[0] // tk
            fold_max(0)
            for c in range(n_chunks):
                cur, nxt, more = c % 2, (c + 1) % 2, c + 1 < n_chunks
                if more:
                    s_ref[nxt] = scores(c + 1)
                alpha = al_ref[...]
                pr = jnp.exp2(s_ref[cur] - m_ref[...])
                l_ref[...] = alpha * l_ref[...] + jnp.sum(pr, axis=0, keepdims=True)
                pv = _dot(vt_ref[:, c * tk:(c + 1) * tk], pr.astype(BF16))
                if more:
                    fold_max(nxt)
                acc_ref[...] = alpha * acc_ref[...] + pv

        if latents:
            pl.when(pl.program_id(2) + t >= 0)(latent_chunks)

        o = acc_ref[...] * (1.0 / l_ref[...])
        o = o[:, :tq] - lam_ref[0] * o[:, tq:]
        o = o * lax.rsqrt(jnp.mean(o * o, axis=0, keepdims=True) + EPS)
        o = o * jnp.tile(sub_ref[...], (1, tq // LANES)) * out_scale
        o_ref[rows, :] = o.T.astype(o_ref.dtype)


def _attn(qkv, lam, subln, lam_init, lay, need_ctx):
    hd = subln.shape[-1]
    heads = qkv.shape[1] // (3 * hd)
    d = heads * hd
    vt = qkv[:, 2 * d:].T
    ctx_block = lay.batch * lay.seq // lay.ctx
    sub = jnp.broadcast_to(subln.reshape(hd, 1), (hd, LANES))
    smem = pl.BlockSpec(memory_space=pltpu.SMEM)
    ctx_specs = [pl.BlockSpec((lay.ctx, hd), lambda b, h, qi: (ctx_block + b, heads + h)),
                 pl.BlockSpec((hd, lay.ctx), lambda b, h, qi: (h, ctx_block + b))]
    sub_spec = pl.BlockSpec((hd, LANES), lambda b, h, qi: (0, 0))

    def scratch(tq, tk):
        return ([pltpu.VMEM((hd, 2 * tq), BF16), pltpu.VMEM((2, tk, 2 * tq), F32)]
                + [pltpu.VMEM((1, 2 * tq), F32)] * 3 + [pltpu.VMEM((hd, 2 * tq), F32)])

    tq = _pick(lay.seq, (256, 128))
    tk = _pick(lay.seq, (512, 256, 128))
    q_tiles = 2 if (lay.seq // tq) % 2 == 0 else 1
    n_q = lay.seq // (tq * q_tiles)
    a_lat = pl.pallas_call(
        functools.partial(_attn_kernel, latents=True, tk=tk, q_tiles=q_tiles, out_scale=1.0 - lam_init),
        grid=(lay.batch, heads, n_q),
        in_specs=[smem, pl.BlockSpec((tq * q_tiles, hd), lambda b, h, qi: (b * n_q + qi, h))] + ctx_specs
        + [pl.BlockSpec((lay.seq, hd), lambda b, h, qi: (b, heads + h)),
           pl.BlockSpec((hd, lay.seq), lambda b, h, qi: (h, b)), sub_spec],
        out_specs=pl.BlockSpec((tq * q_tiles, hd), lambda b, h, qi: (b * n_q + qi, h)),
        out_shape=jax.ShapeDtypeStruct((lay.batch * lay.seq, d), BF16),
        scratch_shapes=scratch(tq, tk),
        compiler_params=_cparams(("parallel", "parallel", "arbitrary")),
    )(lam, qkv, qkv, vt, qkv, vt, sub)
    if not need_ctx:
        return a_lat, None
    a_ctx = pl.pallas_call(
        functools.partial(_attn_kernel, latents=False, tk=SUBLANES, q_tiles=1, out_scale=1.0 - lam_init),
        grid=(lay.batch, heads, 1),
        in_specs=[smem, pl.BlockSpec((lay.ctx, hd), lambda b, h, qi: (ctx_block + b, h))] + ctx_specs + [sub_spec],
        out_specs=pl.BlockSpec((lay.ctx, hd), lambda b, h, qi: (b, h)),
        out_shape=jax.ShapeDtypeStruct((lay.batch * lay.ctx, d), BF16),
        scratch_shapes=scratch(lay.ctx, SUBLANES),
        compiler_params=_cparams(("parallel", "parallel", "arbitrary")),
    )(lam, qkv, qkv, vt, sub)
    return a_lat, a_ctx


def _attnout_kernel(x_ref, mod_ref, g_ref, a_ref, *rest, d, n_lat_tiles):
    _, _, gate = _mod3(mod_ref, 1, d)
    if len(rest) == 3:
        actx_ref, w_ref, o_ref = rest
        a = jnp.where(pl.program_id(0) < n_lat_tiles, a_ref[...], actx_ref[...])
    else:
        w_ref, o_ref = rest
        a = a_ref[...]
    y = _dot(a, w_ref[...])
    o_ref[...] = x_ref[...] + gate * _rms(y, g_ref[3:4, :])


def _attnout(xt, mod, norm_g, a_lat, a_ctx, w_o, layer, i_odd, lay):
    d = xt.shape[1]
    tm = min(lay.tm, 256)
    lay_m = lay.with_tm(tm)
    n_lat = lay_m.n_lat_tiles
    r = a_lat.shape[0] + (0 if a_ctx is None else a_ctx.shape[0])
    a_specs = [pl.BlockSpec((tm, d), lambda i: (jnp.minimum(i, n_lat - 1), 0))]
    a_args = [a_lat]
    if a_ctx is not None:
        a_specs.append(pl.BlockSpec((tm, d), lambda i: (jnp.maximum(i - n_lat, 0), 0)))
        a_args.append(a_ctx)
    return pl.pallas_call(
        functools.partial(_attnout_kernel, d=d, n_lat_tiles=n_lat),
        grid=(r // tm,),
        in_specs=[pl.BlockSpec((tm, d), lambda i: (i, 0)),
                  pl.BlockSpec((None, None, 1, N_MOD * d), lambda i: (layer, lay_m.mod_row(i), 0, 0)),
                  pl.BlockSpec((None, 6, d), lambda i: (layer, 0, 0))] + a_specs
        + [pl.BlockSpec((None, d, d), lambda i: (i_odd, 0, 0))],
        out_specs=pl.BlockSpec((tm, d), lambda i: (i, 0)),
        out_shape=jax.ShapeDtypeStruct((r, d), F32),
        compiler_params=_cparams(("parallel",)),
    )(xt, mod, norm_g, *a_args, w_o)


class _Layout:
    def __init__(self, batch, seq, ctx, tm, q_scale):
        self.batch, self.seq, self.ctx, self.tm, self.q_scale = batch, seq, ctx, tm, q_scale
        self.tiles_per_seq = seq // tm
        self.n_lat_tiles = batch * self.tiles_per_seq

    def with_tm(self, tm):
        return _Layout(self.batch, self.seq, self.ctx, tm, self.q_scale)

    def mod_row(self, i):
        return jnp.minimum(i // self.tiles_per_seq, self.batch)


def kernel(x, c, ctx, c_ctx, w_mod, b_mod, norm_g, ffn_wg, ffn_wu, ffn_wd, mix_w_in, mix_w_out, s5_lam_re, s5_lam_im, s5_log_step, s5_b_re, s5_b_im, s5_c_re, s5_c_im, s5_d, s5_w_glu, s5_b_glu, conv_w, conv_b, attn_w_qkv, attn_w_o, attn_lambda, attn_subln):
    batch, seq, d = x.shape
    n_ctx = ctx.shape[1]
    depth = w_mod.shape[0]
    hd = attn_subln.shape[-1]
    dh = attn_lambda.shape[-1]
    g, p, h = s5_b_re.shape[2:]
    sw = g * h
    t = S5_CHUNK
    assert batch + 1 <= SUBLANES and conv_w.shape[-1] == sw and mix_w_in.shape[-1] == 4 * sw
    assert g % S5_BLOCK == 0 and (S5_BLOCK * p) % LANES == 0 and (S5_BLOCK * h * t) % LANES == 0
    assert seq % (t * SUBLANES) == 0 and n_ctx % (t * SUBLANES) == 0 and seq % n_ctx == 0 and hd == 2 * dh
    tm = _pick(math.gcd(seq, batch * n_ctx), (512, 256, 128, 64, 32, 16))
    lay = _Layout(batch, seq, n_ctx, tm, dh ** -0.5 * math.log2(math.e))
    r = batch * (seq + n_ctx)

    xt = jnp.concatenate([x.reshape(batch * seq, d), ctx.reshape(batch * n_ctx, d)], axis=0)
    cc = jnp.zeros((SUBLANES, d), F32).at[:batch].set(c).at[batch].set(c_ctx)
    mod = _ada(cc, w_mod, b_mod).reshape(depth, SUBLANES, 1, N_MOD * d)

    wg, wu, wd = ffn_wg.astype(BF16), ffn_wu.astype(BF16), ffn_wd.astype(BF16)
    w_in, w_out, w_glu = mix_w_in.astype(BF16), mix_w_out.astype(BF16), s5_w_glu.astype(BF16)
    w_qkv, w_o = attn_w_qkv.astype(BF16), attn_w_o.astype(BF16)

    cos, sin = _rope_tables(seq)
    pos = jnp.concatenate([jnp.tile(jnp.arange(seq), batch), jnp.tile(jnp.arange(n_ctx), batch)])
    last = jnp.concatenate([jnp.full((batch * seq,), seq - 1), jnp.full((batch * n_ctx,), n_ctx - 1)])
    keep_prev = (pos != 0).astype(F32)[:, None]
    keep_next = (pos != last).astype(F32)[:, None]
    nc = r // t
    mats = jax.vmap(functools.partial(_s5_matrices, t=t))(s5_lam_re, s5_lam_im, s5_log_step, s5_b_re, s5_b_im,
                                                          s5_c_re, s5_c_im)

    for layer in range(depth):
        i = layer // 2
        xt = _ffn(xt, mod, norm_g, wg, wu, wd, layer, 0, 0, lay)
        if layer % 2 == 0:
            pj = _proj(xt, mod, norm_g, w_in, i, layer, lay)
            nb, wb = g // S5_BLOCK, S5_BLOCK * h
            uf = pj[:, :sw].reshape(nc, t, nb, wb).transpose(2, 0, 1, 3).reshape(nb, nc, t * wb)
            ys = _s5(uf, mats, i, lay)
            ys = ys.reshape(nb, nc, t, wb).transpose(1, 2, 0, 3).reshape(r, sw)
            xt = _mixout(xt, mod, norm_g, ys, pj, keep_prev, keep_next, s5_d, w_glu, s5_b_glu, conv_w, conv_b, w_out,
                         layer, i, lay)
        else:
            lam_init = 0.8 - 0.6 * math.exp(-0.3 * layer)
            lv = attn_lambda[i].astype(F32)
            lam = (jnp.exp(jnp.sum(lv[0] * lv[1])) - jnp.exp(jnp.sum(lv[2] * lv[3])) + lam_init).reshape(1)
            qkv = _proj(xt, mod, norm_g, w_qkv, i, layer, lay, rope=(cos, sin, 2 * d, d))
            a_lat, a_ctx = _attn(qkv, lam, attn_subln[i], lam_init, lay, need_ctx=layer < depth - 1)
            xt = _attnout(xt, mod, norm_g, a_lat, a_ctx, w_o, layer, i, lay)
        xt = _ffn(xt, mod, norm_g, wg, wu, wd, layer, 2, 1, lay, rows=batch * seq if layer == depth - 1 else None)
    return xt.reshape(batch, seq, d)
```

```python
import functools
import math

import jax
import jax.numpy as jnp
from jax import lax
from jax.experimental import pallas as pl
from jax.experimental.pallas import tpu as pltpu

F32 = jnp.float32
BF16 = jnp.bfloat16
EPS = 1e-6
N_MOD = 9
GRID_W = 64
ROPE_BASE = 10000.0
S5_CHUNK = 16
S5_BLOCK = 4
SUBLANES = 8
LANES = 128
ROW_CHUNK = 16
ROW_UNROLL = 8
VMEM_LIMIT = 56 * 1024 * 1024


def _cparams(sem):
    return pltpu.CompilerParams(dimension_semantics=sem, vmem_limit_bytes=VMEM_LIMIT)


def _dot(a, b):
    return jnp.dot(a, b, preferred_element_type=F32)


def _rms(x, g):
    return x * lax.rsqrt(jnp.mean(x * x, axis=-1, keepdims=True) + EPS) * g


def _for_row_chunks(n_rows, fn):
    def body(i, carry):
        fn(pl.ds(pl.multiple_of(i * ROW_CHUNK, ROW_CHUNK), ROW_CHUNK))
        return carry
    lax.fori_loop(0, n_rows // ROW_CHUNK, body, 0, unroll=ROW_UNROLL)


def _norm_modulate(x_ref, h_ref, g, scale, shift):
    gs = g * (1.0 + scale)

    def chunk(rows):
        x = x_ref[rows, :]
        h_ref[rows, :] = (_rms(x, gs) + shift).astype(h_ref.dtype)
    _for_row_chunks(x_ref.shape[0], chunk)


def _gated_residual(x_ref, y_ref, o_ref, gg):
    def chunk(rows):
        o_ref[rows, :] = x_ref[rows, :] + _rms(y_ref[rows, :], gg)
    _for_row_chunks(x_ref.shape[0], chunk)


def _pick(n, candidates):
    for c in candidates:
        if n % c == 0:
            return c
    return n


def _ada_kernel(c_ref, w_ref, b_ref, o_ref):
    s = c_ref[...]
    s = s * jax.nn.sigmoid(s)
    o_ref[...] = _dot(s.astype(BF16), w_ref[...].astype(BF16)) + b_ref[...]


def _ada(cc, w_mod, b_mod):
    depth, d, n = w_mod.shape
    tn = _pick(n, (1024, 512, 256, 128))
    return pl.pallas_call(
        _ada_kernel,
        grid=(depth, n // tn),
        in_specs=[pl.BlockSpec((SUBLANES, d), lambda l, j: (0, 0)),
                  pl.BlockSpec((None, d, tn), lambda l, j: (l, 0, j)),
                  pl.BlockSpec((None, 1, tn), lambda l, j: (l, 0, j))],
        out_specs=pl.BlockSpec((None, SUBLANES, tn), lambda l, j: (l, 0, j)),
        out_shape=jax.ShapeDtypeStruct((depth, SUBLANES, n), F32),
        compiler_params=_cparams(("arbitrary", "arbitrary")),
    )(cc, w_mod, b_mod.reshape(depth, 1, n))


def _mod3(mod_ref, sub, d):
    return tuple(mod_ref[:, (3 * sub + k) * d:(3 * sub + k + 1) * d] for k in range(3))


def _ffn_kernel(x_ref, mod_ref, g_ref, wg_ref, wu_ref, wd_ref, o_ref, h_ref, acc_ref, *, sub, d):
    j = pl.program_id(1)
    shift, scale, gate = _mod3(mod_ref, sub, d)

    @pl.when(j == 0)
    def _():
        _norm_modulate(x_ref, h_ref, g_ref[2 * sub:2 * sub + 1, :], scale, shift)
        acc_ref[...] = jnp.zeros_like(acc_ref)

    h = h_ref[...]
    a = _dot(h, wg_ref[...])
    u = _dot(h, wu_ref[...])
    act = (a * jax.nn.sigmoid(a) * u).astype(BF16)
    acc_ref[...] += _dot(act, wd_ref[...])

    @pl.when(j == pl.num_programs(1) - 1)
    def _():
        _gated_residual(x_ref, acc_ref, o_ref, 0.5 * gate * g_ref[2 * sub + 1:2 * sub + 2, :])


def _ffn(xt, mod, norm_g, wg, wu, wd, layer, sub, widx, lay, rows=None):
    d = xt.shape[1]
    r = xt.shape[0] if rows is None else rows
    f = wg.shape[-1]
    tm, tf = lay.tm, _pick(f, (512, 256, 128))
    return pl.pallas_call(
        functools.partial(_ffn_kernel, sub=sub, d=d),
        grid=(r // tm, f // tf),
        in_specs=[pl.BlockSpec((tm, d), lambda i, j: (i, 0)),
                  pl.BlockSpec((None, None, 1, N_MOD * d), lambda i, j: (layer, lay.mod_row(i), 0, 0)),
                  pl.BlockSpec((None, 6, d), lambda i, j: (layer, 0, 0)),
                  pl.BlockSpec((None, None, d, tf), lambda i, j: (layer, widx, 0, j)),
                  pl.BlockSpec((None, None, d, tf), lambda i, j: (layer, widx, 0, j)),
                  pl.BlockSpec((None, None, tf, d), lambda i, j: (layer, widx, j, 0))],
        out_specs=pl.BlockSpec((tm, d), lambda i, j: (i, 0)),
        out_shape=jax.ShapeDtypeStruct((r, d), F32),
        scratch_shapes=[pltpu.VMEM((tm, d), BF16), pltpu.VMEM((tm, d), F32)],
        compiler_params=_cparams(("parallel", "arbitrary")),
    )(xt, mod, norm_g, wg, wu, wd)


def _proj_kernel(x_ref, mod_ref, g_ref, w_ref, *rest, d, tn, n_q, n_rot):
    shift, scale, _ = _mod3(mod_ref, 1, d)
    o_ref, h_ref = rest[-2:]
    _norm_modulate(x_ref, h_ref, g_ref[2:3, :], scale, shift)
    h = h_ref[...]
    lane = lax.broadcasted_iota(jnp.int32, (1, LANES), 1)
    first_half = (lane % 64) < 32
    for c in range(w_ref.shape[1] // tn):
        col = c * tn
        y = _dot(h, w_ref[:, col:col + tn])
        if col >= n_rot:
            o_ref[:, col:col + tn] = y.astype(o_ref.dtype)
            continue
        cos_ref, sin_ref = rest[:2] if col < n_q else rest[2:4]
        for cc in range(tn // LANES):
            yc = y[:, cc * LANES:(cc + 1) * LANES]
            partner = jnp.where(first_half, pltpu.roll(yc, LANES - 32, 1), pltpu.roll(yc, 32, 1))
            o_ref[:, col + cc * LANES:col + (cc + 1) * LANES] = (
                yc * cos_ref[...] + partner * sin_ref[...]).astype(o_ref.dtype)


def _proj(xt, mod, norm_g, w, widx, layer, lay, rope=None):
    r, d = xt.shape
    n = w.shape[-1]
    tm = lay.tm
    in_specs = [pl.BlockSpec((tm, d), lambda i: (i, 0)),
                pl.BlockSpec((None, None, 1, N_MOD * d), lambda i: (layer, lay.mod_row(i), 0, 0)),
                pl.BlockSpec((None, 6, d), lambda i: (layer, 0, 0)),
                pl.BlockSpec((None, d, n), lambda i: (widx, 0, 0), pipeline_mode=pl.Buffered(1))]
    if rope is None:
        tn = _pick(n, (1024, 512, 256, 128))
        n_rot = n_q = 0
        tables = ()
    else:
        cos, sin, n_rot, n_q = rope
        tn = _pick(math.gcd(n_q, n), (1024, 512, 256, 128))
        tps = lay.tiles_per_seq
        one, zero = jnp.ones((tm, LANES), F32), jnp.zeros((tm, LANES), F32)
        cos_tab = jnp.concatenate([cos * lay.q_scale, cos, one * lay.q_scale, one])
        sin_tab = jnp.concatenate([sin * lay.q_scale, sin, zero, zero])
        q_map = lambda i: (jnp.where(i < lay.n_lat_tiles, i % tps, 2 * tps), 0)
        k_map = lambda i: (jnp.where(i < lay.n_lat_tiles, tps + i % tps, 2 * tps + 1), 0)
        tables = (cos_tab, sin_tab, cos_tab, sin_tab)
        in_specs += [pl.BlockSpec((tm, LANES), m) for m in (q_map, q_map, k_map, k_map)]
    return pl.pallas_call(
        functools.partial(_proj_kernel, d=d, tn=tn, n_q=n_q, n_rot=n_rot),
        grid=(r // tm,),
        in_specs=in_specs,
        out_specs=pl.BlockSpec((tm, n), lambda i: (i, 0)),
        out_shape=jax.ShapeDtypeStruct((r, n), BF16),
        scratch_shapes=[pltpu.VMEM((tm, d), BF16)],
        compiler_params=_cparams(("parallel",)),
    )(xt, mod, norm_g, w, *tables)


def _rope_tables(n):
    rows = n // GRID_W
    row = jnp.repeat(jnp.arange(rows), GRID_W).astype(F32)
    col = jnp.tile(jnp.arange(GRID_W), rows).astype(F32)
    pairs = 16
    freqs = jnp.power(ROPE_BASE, -jnp.arange(pairs, dtype=F32) / pairs)
    ang = jnp.concatenate([row[:, None] * freqs, col[:, None] * freqs], axis=-1)
    cos, sin = jnp.cos(ang), jnp.sin(ang)
    return jnp.tile(cos, (1, 4)), jnp.concatenate([-sin, sin, -sin, sin], axis=-1)


def _s5_matrices(lam_re, lam_im, log_step, b_re, b_im, c_re, c_im, t):
    hp = lax.Precision.HIGHEST
    lam_re, lam_im = lam_re.astype(F32), lam_im.astype(F32)
    dt = jnp.exp(log_step.astype(F32))[..., None]
    lr, li = lam_re * dt, lam_im * dt
    mag = jnp.exp(lr)
    a_re, a_im = mag * jnp.cos(li), mag * jnp.sin(li)
    inv = 1.0 / (lam_re * lam_re + lam_im * lam_im)
    co_re = ((a_re - 1.0) * lam_re + a_im * lam_im) * inv
    co_im = (a_im * lam_re - (a_re - 1.0) * lam_im) * inv
    b_re, b_im = b_re.astype(F32), b_im.astype(F32)
    bb_re = co_re[..., None] * b_re - co_im[..., None] * b_im
    bb_im = co_re[..., None] * b_im + co_im[..., None] * b_re
    c_re, c_im = c_re.astype(F32), c_im.astype(F32)
    tau = jnp.arange(t + 1, dtype=F32)[:, None, None, None]
    pmag = jnp.exp(tau * lr[None])
    pw_re, pw_im = pmag * jnp.cos(tau * li[None]), pmag * jnp.sin(tau * li[None])
    cp_re = c_re[None] * pw_re[:, :, :, None, :] - c_im[None] * pw_im[:, :, :, None, :]
    cp_im = c_re[None] * pw_im[:, :, :, None, :] + c_im[None] * pw_re[:, :, :, None, :]
    kern = jnp.einsum('tdghp,dgpk->tdghk', jnp.concatenate([cp_re, -cp_im], axis=-1),
                      jnp.concatenate([bb_re, bb_im], axis=-2), precision=hp)
    g, p, h = b_re.shape[1:]
    s_idx = jnp.arange(t)[:, None]
    t_idx = jnp.arange(t)[None, :]

    gb = S5_BLOCK
    nb = g // gb
    th, wide = t * h, t * gb * h
    col = jnp.arange(wide)
    row = jnp.arange(th)
    grp = jnp.arange(gb)[:, None, None]
    sel_th = ((row[None, :, None] // h == col[None, None, :] // (gb * h)) & (row[None, :, None] % h == col[None, None, :] % h)
              & ((col[None, None, :] // h) % gb == grp)).astype(BF16)
    colp = jnp.arange(gb * p)
    sel_p = ((jnp.arange(p)[None, :, None] == colp[None, None, :] % p) & (colp[None, None, :] // p == grp)).astype(BF16)

    def widen(m, sel):
        return jnp.einsum('qerc,ecy->qery', m.astype(BF16).astype(F32), sel.astype(F32)).astype(BF16)

    def interleave(m):
        return m.reshape(nb, gb, t, h, m.shape[-1]).transpose(0, 2, 1, 3, 4).reshape(nb, wide, m.shape[-1])

    def toeplitz(k_dir, lag):
        m = jnp.where((lag >= 0)[:, :, None, None, None], k_dir[jnp.clip(lag, 0, t)], 0.0)
        return m.transpose(2, 0, 4, 1, 3).reshape(nb, gb, th, th)

    mk = interleave(widen(toeplitz(kern[:, 0], t_idx - s_idx) + toeplitz(kern[:, 1], s_idx - t_idx), sel_th))

    def drive(d, exps):
        wr, wi = pw_re[exps, d], pw_im[exps, d]
        re = wr[..., None] * bb_re[d][None] - wi[..., None] * bb_im[d][None]
        im = wr[..., None] * bb_im[d][None] + wi[..., None] * bb_re[d][None]
        f = lambda m: interleave(widen(m.reshape(t, nb, gb, p, h).transpose(1, 2, 0, 4, 3).reshape(nb, gb, th, p), sel_p))
        return jnp.concatenate([f(re), f(im)], axis=-1)

    def read(d, exps):
        f = lambda m: widen(m.reshape(t, nb, gb, h, p).transpose(1, 2, 4, 0, 3).reshape(nb, gb, p, th),
                            sel_th).reshape(nb, gb * p, wide)
        return jnp.concatenate([f(cp_re[exps, d]), f(-cp_im[exps, d])], axis=1)

    ar = jnp.arange(t)
    ms_f, ms_r = drive(0, t - 1 - ar), drive(1, ar)
    mo_f, mo_r = read(0, ar + 1), read(1, t - ar)
    at = jnp.stack([pw_re[t, 0], pw_im[t, 0], pw_re[t, 1], pw_im[t, 1]], axis=1)
    at = at.reshape(nb, gb, 4, p).transpose(0, 2, 1, 3).reshape(nb, 4, gb * p)
    return mk, ms_f, ms_r, mo_f, mo_r, at


def _s5_kernel(uf_ref, mk_ref, msf_ref, msr_ref, mof_ref, mor_ref, a_ref, y_ref, sf_ref, sr_ref, hf_ref, hr_ref, *,
               batch, lat_tiles, ctx_tiles):
    w = a_ref.shape[-1]
    uf = uf_ref[...]
    sf_ref[...] = _dot(uf, msf_ref[...])
    sr_ref[...] = _dot(uf, msr_ref[...])
    a = a_ref[...]
    af = (a[0:1], a[1:2])
    ab = (a[2:3], a[3:4])
    row_id = lax.broadcasted_iota(jnp.int32, (SUBLANES, 2 * w), 0)

    def tile_step(s_ref, hp_ref, tile, state, coef, rows):
        off = pl.multiple_of(tile * SUBLANES, SUBLANES)
        s = s_ref[pl.ds(off, SUBLANES), :]
        hre, him = state
        are, aim = coef
        prev = jnp.zeros((SUBLANES, 2 * w), F32)
        for i in rows:
            row = jnp.concatenate([hre, him], axis=1)
            prev = jnp.where(row_id == i, jnp.broadcast_to(row, prev.shape), prev)
            sre, sim = s[i:i + 1, :w], s[i:i + 1, w:]
            hre, him = are * hre - aim * him + sre, are * him + aim * hre + sim
        hp_ref[pl.ds(off, SUBLANES), :] = prev
        return hre, him

    zero = jnp.zeros((1, w), F32)
    state = tuple((zero, zero) for _ in range(2 * batch))
    asc, desc = tuple(range(SUBLANES)), tuple(reversed(range(SUBLANES)))

    def phase(first_tile, n_tiles, state):
        def body(k, st):
            out = []
            for b in range(batch):
                out.append(tile_step(sf_ref, hf_ref, first_tile(b) + k, st[2 * b], af, asc))
                out.append(tile_step(sr_ref, hr_ref, first_tile(b) + n_tiles - 1 - k, st[2 * b + 1], ab, desc))
            return tuple(out)
        return lax.fori_loop(0, n_tiles, body, state)

    state = phase(lambda b: batch * lat_tiles + b * ctx_tiles, ctx_tiles, state)
    phase(lambda b: b * lat_tiles, lat_tiles, state)

    y_ref[...] = (_dot(uf, mk_ref[...]) + _dot(hf_ref[...].astype(BF16), mof_ref[...])
                  + _dot(hr_ref[...].astype(BF16), mor_ref[...]))


def _s5(uf, mats, i_even, lay):
    mk, ms_f, ms_r, mo_f, mo_r, at = mats
    nb, nc, wu = uf.shape
    ws = ms_f.shape[-1]
    t = S5_CHUNK
    full = lambda *shape: pl.BlockSpec((None,) + shape, lambda q: (q,) + (0,) * len(shape))
    mat = lambda *shape: pl.BlockSpec((None, None) + shape, lambda q: (i_even, q) + (0,) * len(shape))
    return pl.pallas_call(
        functools.partial(_s5_kernel, batch=lay.batch, lat_tiles=lay.seq // (t * SUBLANES),
                          ctx_tiles=lay.ctx // (t * SUBLANES)),
        grid=(nb,),
        in_specs=[full(nc, wu), mat(wu, wu), mat(wu, ws), mat(wu, ws), mat(ws, wu), mat(ws, wu),
                  mat(4, ws // 2)],
        out_specs=full(nc, wu),
        out_shape=jax.ShapeDtypeStruct((nb, nc, wu), F32),
        scratch_shapes=[pltpu.VMEM((nc, ws), F32)] * 4,
        compiler_params=_cparams(("parallel",)),
    )(uf, mk, ms_f, ms_r, mo_f, mo_r, at)


def _gelu_tanh(x):
    return 0.5 * x * (1.0 + jnp.tanh(math.sqrt(2.0 / math.pi) * (x + 0.044715 * (x * x * x))))


def _mixout_kernel(x_ref, mod_ref, g_ref, y_ref, u_ref, bg_ref, cg_ref, v_ref, cgp_ref, vp_ref, cgn_ref, vn_ref,
                   kp_ref, kn_ref, d_ref, wglu_ref, bglu_ref, cw_ref, cb_ref, wout_ref, o_ref, *, d, halo):
    _, _, gate = _mod3(mod_ref, 1, d)
    sw = u_ref.shape[1]
    tm = x_ref.shape[0]
    z = _gelu_tanh(y_ref[...] + d_ref[...] * u_ref[...].astype(F32))
    s5 = z * jax.nn.sigmoid(_dot(z.astype(BF16), wglu_ref[...]) + bglu_ref[...])
    gv = cg_ref[...].astype(F32) * v_ref[...].astype(F32)
    gv_before = cgp_ref[halo - 1:halo, :].astype(F32) * vp_ref[halo - 1:halo, :].astype(F32)
    gv_after = cgn_ref[0:1, :].astype(F32) * vn_ref[0:1, :].astype(F32)
    row = lax.broadcasted_iota(jnp.int32, (tm, 1), 0)
    prev = jnp.where(row == 0, gv_before, pltpu.roll(gv, 1, 0)) * kp_ref[...]
    nxt = jnp.where(row == tm - 1, gv_after, pltpu.roll(gv, tm - 1, 0)) * kn_ref[...]
    conv = cw_ref[0:1, :] * prev + cw_ref[1:2, :] * gv + cw_ref[2:3, :] * nxt + cb_ref[...]
    conv = bg_ref[...].astype(F32) * conv
    y = _dot(s5.astype(BF16), wout_ref[:sw, :]) + _dot(conv.astype(BF16), wout_ref[sw:, :])
    o_ref[...] = x_ref[...] + gate * _rms(y, g_ref[3:4, :])


def _mixout(xt, mod, norm_g, ys, p, keep_prev, keep_next, s5_d, w_glu, b_glu, conv_w, conv_b, w_out, layer, i_even,
            lay):
    r, d = xt.shape
    sw = ys.shape[1]
    tm = min(lay.tm, 256)
    halo = 16
    hb = tm // halo
    last_halo = r // halo - 1
    col = lambda c: pl.BlockSpec((tm, sw), lambda i: (i, c))
    before = lambda c: pl.BlockSpec((halo, sw), lambda i: (jnp.maximum(i * hb - 1, 0), c))
    after = lambda c: pl.BlockSpec((halo, sw), lambda i: (jnp.minimum((i + 1) * hb, last_halo), c))
    vec = lambda n: pl.BlockSpec((None, 1, n), lambda i: (i_even, 0, 0))
    lay_m = lay.with_tm(tm)
    return pl.pallas_call(
        functools.partial(_mixout_kernel, d=d, halo=halo),
        grid=(r // tm,),
        in_specs=[pl.BlockSpec((tm, d), lambda i: (i, 0)),
                  pl.BlockSpec((None, None, 1, N_MOD * d), lambda i: (layer, lay_m.mod_row(i), 0, 0)),
                  pl.BlockSpec((None, 6, d), lambda i: (layer, 0, 0)),
                  pl.BlockSpec((tm, sw), lambda i: (i, 0)),
                  col(0), col(1), col(2), col(3), before(2), before(3), after(2), after(3),
                  pl.BlockSpec((tm, 1), lambda i: (i, 0)),
                  pl.BlockSpec((tm, 1), lambda i: (i, 0)),
                  vec(sw),
                  pl.BlockSpec((None, sw, sw), lambda i: (i_even, 0, 0)),
                  vec(sw),
                  pl.BlockSpec((None, 3, sw), lambda i: (i_even, 0, 0)),
                  vec(sw),
                  pl.BlockSpec((None, 2 * sw, d), lambda i: (i_even, 0, 0))],
        out_specs=pl.BlockSpec((tm, d), lambda i: (i, 0)),
        out_shape=jax.ShapeDtypeStruct((r, d), F32),
        compiler_params=_cparams(("parallel",)),
    )(xt, mod, norm_g, ys, p, p, p, p, p, p, p, p, keep_prev, keep_next,
      s5_d.reshape(s5_d.shape[0], 1, sw), w_glu, b_glu.reshape(b_glu.shape[0], 1, sw), conv_w,
      conv_b.reshape(conv_b.shape[0], 1, sw), w_out)


def _attn_kernel(lam_ref, q_ref, kc_ref, vct_ref, *rest, latents, tk, q_tiles, out_scale):
    if latents:
        k_ref, vt_ref = rest[:2]
        rest = rest[2:]
    sub_ref, o_ref, q2_ref, s_ref, m_ref, al_ref, l_ref, acc_ref = rest
    hd = q_ref.shape[1]
    tq = q_ref.shape[0] // q_tiles
    for t in range(q_tiles):
        rows = slice(t * tq, (t + 1) * tq)
        qt = q_ref[rows, :].astype(F32).T
        row = lax.broadcasted_iota(jnp.int32, (hd, tq), 0)
        q2_ref[:, :tq] = jnp.where(row < hd // 2, qt, 0.0).astype(BF16)
        q2_ref[:, tq:] = jnp.where(row >= hd // 2, qt, 0.0).astype(BF16)

        def scores(c):
            return _dot(k_ref[c * tk:(c + 1) * tk, :], q2_ref[...])

        if latents:
            s_ref[0] = scores(0)

        s = _dot(kc_ref[...], q2_ref[...])
        m0 = jnp.max(s, axis=0, keepdims=True)
        pr = jnp.exp2(s - m0)
        m_ref[...] = m0
        l_ref[...] = jnp.sum(pr, axis=0, keepdims=True)
        acc_ref[...] = _dot(vct_ref[...], pr.astype(BF16))

        def fold_max(buf):
            m_prev = m_ref[...]
            m_new = jnp.maximum(m_prev, jnp.max(s_ref[buf], axis=0, keepdims=True))
            al_ref[...] = jnp.exp2(m_prev - m_new)
            m_ref[...] = m_new

        def latent_chunks():
            n_chunks = k_ref.shape[0] // tk
            fold_max(0)
            for c in range(n_chunks):
                cur, nxt, more = c % 2, (c + 1) % 2, c + 1 < n_chunks
                if more:
                    s_ref[nxt] = scores(c + 1)
                alpha = al_ref[...]
                pr = jnp.exp2(s_ref[cur] - m_ref[...])
                l_ref[...] = alpha * l_ref[...] + jnp.sum(pr, axis=0, keepdims=True)
                pv = _dot(vt_ref[:, c * tk:(c + 1) * tk], pr.astype(BF16))
                if more:
                    fold_max(nxt)
                acc_ref[...] = alpha * acc_ref[...] + pv

        if latents:
            pl.when(pl.program_id(2) + t >= 0)(latent_chunks)

        o = acc_ref[...] * (1.0 / l_ref[...])
        o = o[:, :tq] - lam_ref[0] * o[:, tq:]
        o = o * lax.rsqrt(jnp.mean(o * o, axis=0, keepdims=True) + EPS)
        o = o * jnp.tile(sub_ref[...], (1, tq // LANES)) * out_scale
        o_ref[rows, :] = o.T.astype(o_ref.dtype)


def _attn(qkv, lam, subln, lam_init, lay, need_ctx):
    hd = subln.shape[-1]
    heads = qkv.shape[1] // (3 * hd)
    d = heads * hd
    vt = qkv[:, 2 * d:].T
    ctx_block = lay.batch * lay.seq // lay.ctx
    sub = jnp.broadcast_to(subln.reshape(hd, 1), (hd, LANES))
    smem = pl.BlockSpec(memory_space=pltpu.SMEM)
    ctx_specs = [pl.BlockSpec((lay.ctx, hd), lambda b, h, qi: (ctx_block + b, heads + h)),
                 pl.BlockSpec((hd, lay.ctx), lambda b, h, qi: (h, ctx_block + b))]
    sub_spec = pl.BlockSpec((hd, LANES), lambda b, h, qi: (0, 0))

    def scratch(tq, tk):
        return ([pltpu.VMEM((hd, 2 * tq), BF16), pltpu.VMEM((2, tk, 2 * tq), F32)]
                + [pltpu.VMEM((1, 2 * tq), F32)] * 3 + [pltpu.VMEM((hd, 2 * tq), F32)])

    tq = _pick(lay.seq, (256, 128))
    tk = _pick(lay.seq, (512, 256, 128))
    q_tiles = _pick(lay.seq // tq, (4, 2, 1))
    n_q = lay.seq // (tq * q_tiles)
    a_lat = pl.pallas_call(
        functools.partial(_attn_kernel, latents=True, tk=tk, q_tiles=q_tiles, out_scale=1.0 - lam_init),
        grid=(lay.batch, heads, n_q),
        in_specs=[smem, pl.BlockSpec((tq * q_tiles, hd), lambda b, h, qi: (b * n_q + qi, h))] + ctx_specs
        + [pl.BlockSpec((lay.seq, hd), lambda b, h, qi: (b, heads + h)),
           pl.BlockSpec((hd, lay.seq), lambda b, h, qi: (h, b)), sub_spec],
        out_specs=pl.BlockSpec((tq * q_tiles, hd), lambda b, h, qi: (b * n_q + qi, h)),
        out_shape=jax.ShapeDtypeStruct((lay.batch * lay.seq, d), BF16),
        scratch_shapes=scratch(tq, tk),
        compiler_params=_cparams(("parallel", "parallel", "arbitrary")),
    )(lam, qkv, qkv, vt, qkv, vt, sub)
    if not need_ctx:
        return a_lat, None
    a_ctx = pl.pallas_call(
        functools.partial(_attn_kernel, latents=False, tk=SUBLANES, q_tiles=1, out_scale=1.0 - lam_init),
        grid=(lay.batch, heads, 1),
        in_specs=[smem, pl.BlockSpec((lay.ctx, hd), lambda b, h, qi: (ctx_block + b, h))] + ctx_specs + [sub_spec],
        out_specs=pl.BlockSpec((lay.ctx, hd), lambda b, h, qi: (b, h)),
        out_shape=jax.ShapeDtypeStruct((lay.batch * lay.ctx, d), BF16),
        scratch_shapes=scratch(lay.ctx, SUBLANES),
        compiler_params=_cparams(("parallel", "parallel", "arbitrary")),
    )(lam, qkv, qkv, vt, sub)
    return a_lat, a_ctx


def _attnout_kernel(x_ref, mod_ref, g_ref, a_ref, *rest, d, n_lat_tiles):
    _, _, gate = _mod3(mod_ref, 1, d)
    if len(rest) == 3:
        actx_ref, w_ref, o_ref = rest
        a = jnp.where(pl.program_id(0) < n_lat_tiles, a_ref[...], actx_ref[...])
    else:
        w_ref, o_ref = rest
        a = a_ref[...]
    y = _dot(a, w_ref[...])
    o_ref[...] = x_ref[...] + gate * _rms(y, g_ref[3:4, :])


def _attnout(xt, mod, norm_g, a_lat, a_ctx, w_o, layer, i_odd, lay):
    d = xt.shape[1]
    tm = min(lay.tm, 256)
    lay_m = lay.with_tm(tm)
    n_lat = lay_m.n_lat_tiles
    r = a_lat.shape[0] + (0 if a_ctx is None else a_ctx.shape[0])
    a_specs = [pl.BlockSpec((tm, d), lambda i: (jnp.minimum(i, n_lat - 1), 0))]
    a_args = [a_lat]
    if a_ctx is not None:
        a_specs.append(pl.BlockSpec((tm, d), lambda i: (jnp.maximum(i - n_lat, 0), 0)))
        a_args.append(a_ctx)
    return pl.pallas_call(
        functools.partial(_attnout_kernel, d=d, n_lat_tiles=n_lat),
        grid=(r // tm,),
        in_specs=[pl.BlockSpec((tm, d), lambda i: (i, 0)),
                  pl.BlockSpec((None, None, 1, N_MOD * d), lambda i: (layer, lay_m.mod_row(i), 0, 0)),
                  pl.BlockSpec((None, 6, d), lambda i: (layer, 0, 0))] + a_specs
        + [pl.BlockSpec((None, d, d), lambda i: (i_odd, 0, 0))],
        out_specs=pl.BlockSpec((tm, d), lambda i: (i, 0)),
        out_shape=jax.ShapeDtypeStruct((r, d), F32),
        compiler_params=_cparams(("parallel",)),
    )(xt, mod, norm_g, *a_args, w_o)


class _Layout:
    def __init__(self, batch, seq, ctx, tm, q_scale):
        self.batch, self.seq, self.ctx, self.tm, self.q_scale = batch, seq, ctx, tm, q_scale
        self.tiles_per_seq = seq // tm
        self.n_lat_tiles = batch * self.tiles_per_seq

    def with_tm(self, tm):
        return _Layout(self.batch, self.seq, self.ctx, tm, self.q_scale)

    def mod_row(self, i):
        return jnp.minimum(i // self.tiles_per_seq, self.batch)


def kernel(x, c, ctx, c_ctx, w_mod, b_mod, norm_g, ffn_wg, ffn_wu, ffn_wd, mix_w_in, mix_w_out, s5_lam_re, s5_lam_im, s5_log_step, s5_b_re, s5_b_im, s5_c_re, s5_c_im, s5_d, s5_w_glu, s5_b_glu, conv_w, conv_b, attn_w_qkv, attn_w_o, attn_lambda, attn_subln):
    batch, seq, d = x.shape
    n_ctx = ctx.shape[1]
    depth = w_mod.shape[0]
    hd = attn_subln.shape[-1]
    dh = attn_lambda.shape[-1]
    g, p, h = s5_b_re.shape[2:]
    sw = g * h
    t = S5_CHUNK
    assert batch + 1 <= SUBLANES and conv_w.shape[-1] == sw and mix_w_in.shape[-1] == 4 * sw
    assert g % S5_BLOCK == 0 and (S5_BLOCK * p) % LANES == 0 and (S5_BLOCK * h * t) % LANES == 0
    assert seq % (t * SUBLANES) == 0 and n_ctx % (t * SUBLANES) == 0 and seq % n_ctx == 0 and hd == 2 * dh
    tm = _pick(math.gcd(seq, batch * n_ctx), (512, 256, 128, 64, 32, 16))
    lay = _Layout(batch, seq, n_ctx, tm, dh ** -0.5 * math.log2(math.e))
    r = batch * (seq + n_ctx)

    xt = jnp.concatenate([x.reshape(batch * seq, d), ctx.reshape(batch * n_ctx, d)], axis=0)
    cc = jnp.zeros((SUBLANES, d), F32).at[:batch].set(c).at[batch].set(c_ctx)
    mod = _ada(cc, w_mod, b_mod).reshape(depth, SUBLANES, 1, N_MOD * d)

    wg, wu, wd = ffn_wg.astype(BF16), ffn_wu.astype(BF16), ffn_wd.astype(BF16)
    w_in, w_out, w_glu = mix_w_in.astype(BF16), mix_w_out.astype(BF16), s5_w_glu.astype(BF16)
    w_qkv, w_o = attn_w_qkv.astype(BF16), attn_w_o.astype(BF16)

    cos, sin = _rope_tables(seq)
    pos = jnp.concatenate([jnp.tile(jnp.arange(seq), batch), jnp.tile(jnp.arange(n_ctx), batch)])
    last = jnp.concatenate([jnp.full((batch * seq,), seq - 1), jnp.full((batch * n_ctx,), n_ctx - 1)])
    keep_prev = (pos != 0).astype(F32)[:, None]
    keep_next = (pos != last).astype(F32)[:, None]
    nc = r // t
    mats = jax.vmap(functools.partial(_s5_matrices, t=t))(s5_lam_re, s5_lam_im, s5_log_step, s5_b_re, s5_b_im,
                                                          s5_c_re, s5_c_im)

    for layer in range(depth):
        i = layer // 2
        xt = _ffn(xt, mod, norm_g, wg, wu, wd, layer, 0, 0, lay)
        if layer % 2 == 0:
            pj = _proj(xt, mod, norm_g, w_in, i, layer, lay)
            nb, wb = g // S5_BLOCK, S5_BLOCK * h
            uf = pj[:, :sw].reshape(nc, t, nb, wb).transpose(2, 0, 1, 3).reshape(nb, nc, t * wb)
            ys = _s5(uf, mats, i, lay)
            ys = ys.reshape(nb, nc, t, wb).transpose(1, 2, 0, 3).reshape(r, sw)
            xt = _mixout(xt, mod, norm_g, ys, pj, keep_prev, keep_next, s5_d, w_glu, s5_b_glu, conv_w, conv_b, w_out,
                         layer, i, lay)
        else:
            lam_init = 0.8 - 0.6 * math.exp(-0.3 * layer)
            lv = attn_lambda[i].astype(F32)
            lam = (jnp.exp(jnp.sum(lv[0] * lv[1])) - jnp.exp(jnp.sum(lv[2] * lv[3])) + lam_init).reshape(1)
            qkv = _proj(xt, mod, norm_g, w_qkv, i, layer, lay, rope=(cos, sin, 2 * d, d))
            a_lat, a_ctx = _attn(qkv, lam, attn_subln[i], lam_init, lay, need_ctx=layer < depth - 1)
            xt = _attnout(xt, mod, norm_g, a_lat, a_ctx, w_o, layer, i, lay)
        xt = _ffn(xt, mod, norm_g, wg, wu, wd, layer, 2, 1, lay, rows=batch * seq if layer == depth - 1 else None)
    return xt.reshape(batch, seq, d)
```

```python
import functools
import math

import jax
import jax.numpy as jnp
from jax import lax
from jax.experimental import pallas as pl
from jax.experimental.pallas import tpu as pltpu

F32 = jnp.float32
BF16 = jnp.bfloat16
EPS = 1e-6
N_MOD = 9
GRID_W = 64
ROPE_BASE = 10000.0
S5_CHUNK = 16
S5_BLOCK = 4
SUBLANES = 8
LANES = 128
ROW_CHUNK = 16
ROW_UNROLL = 8
VMEM_LIMIT = 56 * 1024 * 1024


def _cparams(sem):
    return pltpu.CompilerParams(dimension_semantics=sem, vmem_limit_bytes=VMEM_LIMIT)


def _dot(a, b):
    return jnp.dot(a, b, preferred_element_type=F32)


def _rms(x, g):
    return x * lax.rsqrt(jnp.mean(x * x, axis=-1, keepdims=True) + EPS) * g


def _for_row_chunks(n_rows, fn):
    def body(i, carry):
        fn(pl.ds(pl.multiple_of(i * ROW_CHUNK, ROW_CHUNK), ROW_CHUNK))
        return carry
    lax.fori_loop(0, n_rows // ROW_CHUNK, body, 0, unroll=ROW_UNROLL)


def _norm_modulate(x_ref, h_ref, g, scale, shift):
    gs = g * (1.0 + scale)

    def chunk(rows):
        x = x_ref[rows, :]
        h_ref[rows, :] = (_rms(x, gs) + shift).astype(h_ref.dtype)
    _for_row_chunks(x_ref.shape[0], chunk)


def _gated_residual(x_ref, y_ref, o_ref, gg):
    def chunk(rows):
        o_ref[rows, :] = x_ref[rows, :] + _rms(y_ref[rows, :], gg)
    _for_row_chunks(x_ref.shape[0], chunk)


def _pick(n, candidates):
    for c in candidates:
        if n % c == 0:
            return c
    return n


def _ada_kernel(c_ref, w_ref, b_ref, o_ref):
    s = c_ref[...]
    s = s * jax.nn.sigmoid(s)
    o_ref[...] = _dot(s.astype(BF16), w_ref[...].astype(BF16)) + b_ref[...]


def _ada(cc, w_mod, b_mod):
    depth, d, n = w_mod.shape
    tn = _pick(n, (1024, 512, 256, 128))
    return pl.pallas_call(
        _ada_kernel,
        grid=(depth, n // tn),
        in_specs=[pl.BlockSpec((SUBLANES, d), lambda l, j: (0, 0)),
                  pl.BlockSpec((None, d, tn), lambda l, j: (l, 0, j)),
                  pl.BlockSpec((None, 1, tn), lambda l, j: (l, 0, j))],
        out_specs=pl.BlockSpec((None, SUBLANES, tn), lambda l, j: (l, 0, j)),
        out_shape=jax.ShapeDtypeStruct((depth, SUBLANES, n), F32),
        compiler_params=_cparams(("arbitrary", "arbitrary")),
    )(cc, w_mod, b_mod.reshape(depth, 1, n))


def _mod3(mod_ref, sub, d):
    return tuple(mod_ref[:, (3 * sub + k) * d:(3 * sub + k + 1) * d] for k in range(3))


def _ffn_kernel(x_ref, mod_ref, g_ref, wg_ref, wu_ref, wd_ref, o_ref, h_ref, acc_ref, *, sub, d):
    j = pl.program_id(1)
    shift, scale, gate = _mod3(mod_ref, sub, d)

    @pl.when(j == 0)
    def _():
        _norm_modulate(x_ref, h_ref, g_ref[2 * sub:2 * sub + 1, :], scale, shift)
        acc_ref[...] = jnp.zeros_like(acc_ref)

    h = h_ref[...]
    a = _dot(h, wg_ref[...])
    u = _dot(h, wu_ref[...])
    act = (a * jax.nn.sigmoid(a) * u).astype(BF16)
    acc_ref[...] += _dot(act, wd_ref[...])

    @pl.when(j == pl.num_programs(1) - 1)
    def _():
        _gated_residual(x_ref, acc_ref, o_ref, 0.5 * gate * g_ref[2 * sub + 1:2 * sub + 2, :])


def _ffn(xt, mod, norm_g, wg, wu, wd, layer, sub, widx, lay, rows=None):
    d = xt.shape[1]
    r = xt.shape[0] if rows is None else rows
    f = wg.shape[-1]
    tm, tf = lay.tm, _pick(f, (512, 256, 128))
    return pl.pallas_call(
        functools.partial(_ffn_kernel, sub=sub, d=d),
        grid=(r // tm, f // tf),
        in_specs=[pl.BlockSpec((tm, d), lambda i, j: (i, 0)),
                  pl.BlockSpec((None, None, 1, N_MOD * d), lambda i, j: (layer, lay.mod_row(i), 0, 0)),
                  pl.BlockSpec((None, 6, d), lambda i, j: (layer, 0, 0)),
                  pl.BlockSpec((None, None, d, tf), lambda i, j: (layer, widx, 0, j)),
                  pl.BlockSpec((None, None, d, tf), lambda i, j: (layer, widx, 0, j)),
                  pl.BlockSpec((None, None, tf, d), lambda i, j: (layer, widx, j, 0))],
        out_specs=pl.BlockSpec((tm, d), lambda i, j: (i, 0)),
        out_shape=jax.ShapeDtypeStruct((r, d), F32),
        scratch_shapes=[pltpu.VMEM((tm, d), BF16), pltpu.VMEM((tm, d), F32)],
        compiler_params=_cparams(("parallel", "arbitrary")),
    )(xt, mod, norm_g, wg, wu, wd)


def _proj_kernel(x_ref, mod_ref, g_ref, w_ref, *rest, d, tn, n_q, n_rot):
    shift, scale, _ = _mod3(mod_ref, 1, d)
    o_ref, h_ref = rest[-2:]
    _norm_modulate(x_ref, h_ref, g_ref[2:3, :], scale, shift)
    h = h_ref[...]
    lane = lax.broadcasted_iota(jnp.int32, (1, LANES), 1)
    first_half = (lane % 64) < 32
    for c in range(w_ref.shape[1] // tn):
        col = c * tn
        y = _dot(h, w_ref[:, col:col + tn])
        if col >= n_rot:
            o_ref[:, col:col + tn] = y.astype(o_ref.dtype)
            continue
        cos_ref, sin_ref = rest[:2] if col < n_q else rest[2:4]
        for cc in range(tn // LANES):
            yc = y[:, cc * LANES:(cc + 1) * LANES]
            partner = jnp.where(first_half, pltpu.roll(yc, LANES - 32, 1), pltpu.roll(yc, 32, 1))
            o_ref[:, col + cc * LANES:col + (cc + 1) * LANES] = (
                yc * cos_ref[...] + partner * sin_ref[...]).astype(o_ref.dtype)


def _proj(xt, mod, norm_g, w, widx, layer, lay, rope=None):
    r, d = xt.shape
    n = w.shape[-1]
    tm = lay.tm
    in_specs = [pl.BlockSpec((tm, d), lambda i: (i, 0)),
                pl.BlockSpec((None, None, 1, N_MOD * d), lambda i: (layer, lay.mod_row(i), 0, 0)),
                pl.BlockSpec((None, 6, d), lambda i: (layer, 0, 0)),
                pl.BlockSpec((None, d, n), lambda i: (widx, 0, 0), pipeline_mode=pl.Buffered(1))]
    if rope is None:
        tn = _pick(n, (1024, 512, 256, 128))
        n_rot = n_q = 0
        tables = ()
    else:
        cos, sin, n_rot, n_q = rope
        tn = _pick(math.gcd(n_q, n), (1024, 512, 256, 128))
        tps = lay.tiles_per_seq
        one, zero = jnp.ones((tm, LANES), F32), jnp.zeros((tm, LANES), F32)
        cos_tab = jnp.concatenate([cos * lay.q_scale, cos, one * lay.q_scale, one])
        sin_tab = jnp.concatenate([sin * lay.q_scale, sin, zero, zero])
        q_map = lambda i: (jnp.where(i < lay.n_lat_tiles, i % tps, 2 * tps), 0)
        k_map = lambda i: (jnp.where(i < lay.n_lat_tiles, tps + i % tps, 2 * tps + 1), 0)
        tables = (cos_tab, sin_tab, cos_tab, sin_tab)
        in_specs += [pl.BlockSpec((tm, LANES), m) for m in (q_map, q_map, k_map, k_map)]
    return pl.pallas_call(
        functools.partial(_proj_kernel, d=d, tn=tn, n_q=n_q, n_rot=n_rot),
        grid=(r // tm,),
        in_specs=in_specs,
        out_specs=pl.BlockSpec((tm, n), lambda i: (i, 0)),
        out_shape=jax.ShapeDtypeStruct((r, n), BF16),
        scratch_shapes=[pltpu.VMEM((tm, d), BF16)],
        compiler_params=_cparams(("parallel",)),
    )(xt, mod, norm_g, w, *tables)


def _rope_tables(n):
    rows = n // GRID_W
    row = jnp.repeat(jnp.arange(rows), GRID_W).astype(F32)
    col = jnp.tile(jnp.arange(GRID_W), rows).astype(F32)
    pairs = 16
    freqs = jnp.power(ROPE_BASE, -jnp.arange(pairs, dtype=F32) / pairs)
    ang = jnp.concatenate([row[:, None] * freqs, col[:, None] * freqs], axis=-1)
    cos, sin = jnp.cos(ang), jnp.sin(ang)
    return jnp.tile(cos, (1, 4)), jnp.concatenate([-sin, sin, -sin, sin], axis=-1)


def _s5_matrices(lam_re, lam_im, log_step, b_re, b_im, c_re, c_im, t):
    hp = lax.Precision.HIGHEST
    lam_re, lam_im = lam_re.astype(F32), lam_im.astype(F32)
    dt = jnp.exp(log_step.astype(F32))[..., None]
    lr, li = lam_re * dt, lam_im * dt
    mag = jnp.exp(lr)
    a_re, a_im = mag * jnp.cos(li), mag * jnp.sin(li)
    inv = 1.0 / (lam_re * lam_re + lam_im * lam_im)
    co_re = ((a_re - 1.0) * lam_re + a_im * lam_im) * inv
    co_im = (a_im * lam_re - (a_re - 1.0) * lam_im) * inv
    b_re, b_im = b_re.astype(F32), b_im.astype(F32)
    bb_re = co_re[..., None] * b_re - co_im[..., None] * b_im
    bb_im = co_re[..., None] * b_im + co_im[..., None] * b_re
    c_re, c_im = c_re.astype(F32), c_im.astype(F32)
    tau = jnp.arange(t + 1, dtype=F32)[:, None, None, None]
    pmag = jnp.exp(tau * lr[None])
    pw_re, pw_im = pmag * jnp.cos(tau * li[None]), pmag * jnp.sin(tau * li[None])
    cp_re = c_re[None] * pw_re[:, :, :, None, :] - c_im[None] * pw_im[:, :, :, None, :]
    cp_im = c_re[None] * pw_im[:, :, :, None, :] + c_im[None] * pw_re[:, :, :, None, :]
    kern = jnp.einsum('tdghp,dgpk->tdghk', jnp.concatenate([cp_re, -cp_im], axis=-1),
                      jnp.concatenate([bb_re, bb_im], axis=-2), precision=hp)
    g, p, h = b_re.shape[1:]
    s_idx = jnp.arange(t)[:, None]
    t_idx = jnp.arange(t)[None, :]

    gb = S5_BLOCK
    nb = g // gb
    th, wide = t * h, t * gb * h
    col = jnp.arange(wide)
    row = jnp.arange(th)
    grp = jnp.arange(gb)[:, None, None]
    sel_th = ((row[None, :, None] // h == col[None, None, :] // (gb * h)) & (row[None, :, None] % h == col[None, None, :] % h)
              & ((col[None, None, :] // h) % gb == grp)).astype(BF16)
    colp = jnp.arange(gb * p)
    sel_p = ((jnp.arange(p)[None, :, None] == colp[None, None, :] % p) & (colp[None, None, :] // p == grp)).astype(BF16)

    def widen(m, sel):
        return jnp.einsum('qerc,ecy->qery', m.astype(BF16).astype(F32), sel.astype(F32)).astype(BF16)

    def interleave(m):
        return m.reshape(nb, gb, t, h, m.shape[-1]).transpose(0, 2, 1, 3, 4).reshape(nb, wide, m.shape[-1])

    def toeplitz(k_dir, lag):
        m = jnp.where((lag >= 0)[:, :, None, None, None], k_dir[jnp.clip(lag, 0, t)], 0.0)
        return m.transpose(2, 0, 4, 1, 3).reshape(nb, gb, th, th)

    mk = interleave(widen(toeplitz(kern[:, 0], t_idx - s_idx) + toeplitz(kern[:, 1], s_idx - t_idx), sel_th))

    def drive(d, exps):
        wr, wi = pw_re[exps, d], pw_im[exps, d]
        re = wr[..., None] * bb_re[d][None] - wi[..., None] * bb_im[d][None]
        im = wr[..., None] * bb_im[d][None] + wi[..., None] * bb_re[d][None]
        f = lambda m: interleave(widen(m.reshape(t, nb, gb, p, h).transpose(1, 2, 0, 4, 3).reshape(nb, gb, th, p), sel_p))
        return jnp.concatenate([f(re), f(im)], axis=-1)

    def read(d, exps):
        f = lambda m: widen(m.reshape(t, nb, gb, h, p).transpose(1, 2, 4, 0, 3).reshape(nb, gb, p, th),
                            sel_th).reshape(nb, gb * p, wide)
        return jnp.concatenate([f(cp_re[exps, d]), f(-cp_im[exps, d])], axis=1)

    ar = jnp.arange(t)
    ms_f, ms_r = drive(0, t - 1 - ar), drive(1, ar)
    mo_f, mo_r = read(0, ar + 1), read(1, t - ar)
    at = jnp.stack([pw_re[t, 0], pw_im[t, 0], pw_re[t, 1], pw_im[t, 1]], axis=1)
    at = at.reshape(nb, gb, 4, p).transpose(0, 2, 1, 3).reshape(nb, 4, gb * p)
    return mk, ms_f, ms_r, mo_f, mo_r, at


def _s5_kernel(uf_ref, mk_ref, msf_ref, msr_ref, mof_ref, mor_ref, a_ref, y_ref, sf_ref, sr_ref, hf_ref, hr_ref, *,
               batch, lat_tiles, ctx_tiles):
    w = a_ref.shape[-1]
    uf = uf_ref[...]
    sf_ref[...] = _dot(uf, msf_ref[...])
    sr_ref[...] = _dot(uf, msr_ref[...])
    a = a_ref[...]
    af = (a[0:1], a[1:2])
    ab = (a[2:3], a[3:4])
    row_id = lax.broadcasted_iota(jnp.int32, (SUBLANES, 2 * w), 0)

    def tile_step(s_ref, hp_ref, tile, state, coef, rows):
        off = pl.multiple_of(tile * SUBLANES, SUBLANES)
        s = s_ref[pl.ds(off, SUBLANES), :]
        hre, him = state
        are, aim = coef
        prev = jnp.zeros((SUBLANES, 2 * w), F32)
        for i in rows:
            row = jnp.concatenate([hre, him], axis=1)
            prev = jnp.where(row_id == i, jnp.broadcast_to(row, prev.shape), prev)
            sre, sim = s[i:i + 1, :w], s[i:i + 1, w:]
            hre, him = are * hre - aim * him + sre, are * him + aim * hre + sim
        hp_ref[pl.ds(off, SUBLANES), :] = prev
        return hre, him

    zero = jnp.zeros((1, w), F32)
    state = tuple((zero, zero) for _ in range(2 * batch))
    asc, desc = tuple(range(SUBLANES)), tuple(reversed(range(SUBLANES)))

    def phase(first_tile, n_tiles, state):
        def body(k, st):
            out = []
            for b in range(batch):
                out.append(tile_step(sf_ref, hf_ref, first_tile(b) + k, st[2 * b], af, asc))
                out.append(tile_step(sr_ref, hr_ref, first_tile(b) + n_tiles - 1 - k, st[2 * b + 1], ab, desc))
            return tuple(out)
        return lax.fori_loop(0, n_tiles, body, state)

    state = phase(lambda b: batch * lat_tiles + b * ctx_tiles, ctx_tiles, state)
    phase(lambda b: b * lat_tiles, lat_tiles, state)

    y_ref[...] = (_dot(uf, mk_ref[...]) + _dot(hf_ref[...].astype(BF16), mof_ref[...])
                  + _dot(hr_ref[...].astype(BF16), mor_ref[...]))


def _s5(uf, mats, i_even, lay):
    mk, ms_f, ms_r, mo_f, mo_r, at = mats
    nb, nc, wu = uf.shape
    ws = ms_f.shape[-1]
    t = S5_CHUNK
    full = lambda *shape: pl.BlockSpec((None,) + shape, lambda q: (q,) + (0,) * len(shape))
    mat = lambda *shape: pl.BlockSpec((None, None) + shape, lambda q: (i_even, q) + (0,) * len(shape))
    return pl.pallas_call(
        functools.partial(_s5_kernel, batch=lay.batch, lat_tiles=lay.seq // (t * SUBLANES),
                          ctx_tiles=lay.ctx // (t * SUBLANES)),
        grid=(nb,),
        in_specs=[full(nc, wu), mat(wu, wu), mat(wu, ws), mat(wu, ws), mat(ws, wu), mat(ws, wu),
                  mat(4, ws // 2)],
        out_specs=full(nc, wu),
        out_shape=jax.ShapeDtypeStruct((nb, nc, wu), F32),
        scratch_shapes=[pltpu.VMEM((nc, ws), F32)] * 4,
        compiler_params=_cparams(("parallel",)),
    )(uf, mk, ms_f, ms_r, mo_f, mo_r, at)


def _gelu_tanh(x):
    return 0.5 * x * (1.0 + jnp.tanh(math.sqrt(2.0 / math.pi) * (x + 0.044715 * (x * x * x))))


def _mixout_kernel(x_ref, mod_ref, g_ref, y_ref, u_ref, bg_ref, cg_ref, v_ref, cgp_ref, vp_ref, cgn_ref, vn_ref,
                   kp_ref, kn_ref, d_ref, wglu_ref, bglu_ref, cw_ref, cb_ref, wout_ref, o_ref, *, d, halo):
    _, _, gate = _mod3(mod_ref, 1, d)
    sw = u_ref.shape[1]
    tm = x_ref.shape[0]
    z = _gelu_tanh(y_ref[...] + d_ref[...] * u_ref[...].astype(F32))
    s5 = z * jax.nn.sigmoid(_dot(z.astype(BF16), wglu_ref[...]) + bglu_ref[...])
    gv = cg_ref[...].astype(F32) * v_ref[...].astype(F32)
    gv_before = cgp_ref[halo - 1:halo, :].astype(F32) * vp_ref[halo - 1:halo, :].astype(F32)
    gv_after = cgn_ref[0:1, :].astype(F32) * vn_ref[0:1, :].astype(F32)
    row = lax.broadcasted_iota(jnp.int32, (tm, 1), 0)
    prev = jnp.where(row == 0, gv_before, pltpu.roll(gv, 1, 0)) * kp_ref[...]
    nxt = jnp.where(row == tm - 1, gv_after, pltpu.roll(gv, tm - 1, 0)) * kn_ref[...]
    conv = cw_ref[0:1, :] * prev + cw_ref[1:2, :] * gv + cw_ref[2:3, :] * nxt + cb_ref[...]
    conv = bg_ref[...].astype(F32) * conv
    y = _dot(s5.astype(BF16), wout_ref[:sw, :]) + _dot(conv.astype(BF16), wout_ref[sw:, :])
    o_ref[...] = x_ref[...] + gate * _rms(y, g_ref[3:4, :])


def _mixout(xt, mod, norm_g, ys, p, keep_prev, keep_next, s5_d, w_glu, b_glu, conv_w, conv_b, w_out, layer, i_even,
            lay):
    r, d = xt.shape
    sw = ys.shape[1]
    tm = min(lay.tm, 256)
    halo = 16
    hb = tm // halo
    last_halo = r // halo - 1
    col = lambda c: pl.BlockSpec((tm, sw), lambda i: (i, c))
    before = lambda c: pl.BlockSpec((halo, sw), lambda i: (jnp.maximum(i * hb - 1, 0), c))
    after = lambda c: pl.BlockSpec((halo, sw), lambda i: (jnp.minimum((i + 1) * hb, last_halo), c))
    vec = lambda n: pl.BlockSpec((None, 1, n), lambda i: (i_even, 0, 0))
    lay_m = lay.with_tm(tm)
    return pl.pallas_call(
        functools.partial(_mixout_kernel, d=d, halo=halo),
        grid=(r // tm,),
        in_specs=[pl.BlockSpec((tm, d), lambda i: (i, 0)),
                  pl.BlockSpec((None, None, 1, N_MOD * d), lambda i: (layer, lay_m.mod_row(i), 0, 0)),
                  pl.BlockSpec((None, 6, d), lambda i: (layer, 0, 0)),
                  pl.BlockSpec((tm, sw), lambda i: (i, 0)),
                  col(0), col(1), col(2), col(3), before(2), before(3), after(2), after(3),
                  pl.BlockSpec((tm, 1), lambda i: (i, 0)),
                  pl.BlockSpec((tm, 1), lambda i: (i, 0)),
                  vec(sw),
                  pl.BlockSpec((None, sw, sw), lambda i: (i_even, 0, 0)),
                  vec(sw),
                  pl.BlockSpec((None, 3, sw), lambda i: (i_even, 0, 0)),
                  vec(sw),
                  pl.BlockSpec((None, 2 * sw, d), lambda i: (i_even, 0, 0))],
        out_specs=pl.BlockSpec((tm, d), lambda i: (i, 0)),
        out_shape=jax.ShapeDtypeStruct((r, d), F32),
        compiler_params=_cparams(("parallel",)),
    )(xt, mod, norm_g, ys, p, p, p, p, p, p, p, p, keep_prev, keep_next,
      s5_d.reshape(s5_d.shape[0], 1, sw), w_glu, b_glu.reshape(b_glu.shape[0], 1, sw), conv_w,
      conv_b.reshape(conv_b.shape[0], 1, sw), w_out)


def _attn_kernel(lam_ref, q_ref, kc_ref, vct_ref, *rest, latents, tk, q_tiles, out_scale):
    if latents:
        k_ref, vt_ref = rest[:2]
        rest = rest[2:]
    sub_ref, o_ref, q2_ref, s_ref, m_ref, al_ref, l_ref, acc_ref = rest
    hd = q_ref.shape[1]
    tq = q_ref.shape[0] // q_tiles
    for t in range(q_tiles):
        rows = slice(t * tq, (t + 1) * tq)
        qt = q_ref[rows, :].astype(F32).T
        row = lax.broadcasted_iota(jnp.int32, (hd, tq), 0)
        q2_ref[:, :tq] = jnp.where(row < hd // 2, qt, 0.0).astype(BF16)
        q2_ref[:, tq:] = jnp.where(row >= hd // 2, qt, 0.0).astype(BF16)

        def scores(c):
            return _dot(k_ref[c * tk:(c + 1) * tk, :], q2_ref[...])

        if latents:
            s_ref[0] = scores(0)

        s = _dot(kc_ref[...], q2_ref[...])
        m0 = jnp.max(s, axis=0, keepdims=True)
        pr = jnp.exp2(s - m0)
        m_ref[...] = m0
        acc_ref[...] = _dot(vct_ref[...], pr.astype(BF16))

        def fold_max(buf):
            m_prev = m_ref[...]
            m_new = jnp.maximum(m_prev, jnp.max(s_ref[buf], axis=0, keepdims=True))
            al_ref[...] = jnp.exp2(m_prev - m_new)
            m_ref[...] = m_new

        def latent_chunks():
            n_chunks = k_ref.shape[0] // tk
            fold_max(0)
            for c in range(n_chunks):
                cur, nxt, more = c % 2, (c + 1) % 2, c + 1 < n_chunks
                if more:
                    s_ref[nxt] = scores(c + 1)
                alpha = al_ref[...]
                pr = jnp.exp2(s_ref[cur] - m_ref[...])
                pv = _dot(vt_ref[:, c * tk:(c + 1) * tk], pr.astype(BF16))
                if more:
                    fold_max(nxt)
                acc_ref[...] = alpha * acc_ref[...] + pv

        if latents:
            pl.when(pl.program_id(2) + t >= 0)(latent_chunks)

        accv = acc_ref[...]
        o = accv[:hd] * (1.0 / accv[hd:hd + 1])
        o = o[:, :tq] - lam_ref[0] * o[:, tq:]
        o = o * lax.rsqrt(jnp.mean(o * o, axis=0, keepdims=True) + EPS)
        o = o * jnp.tile(sub_ref[...], (1, tq // LANES)) * out_scale
        o_ref[rows, :] = o.T.astype(o_ref.dtype)


def _attn(qkv, lam, subln, lam_init, lay, need_ctx):
    hd = subln.shape[-1]
    heads = qkv.shape[1] // (3 * hd)
    d = heads * hd
    vt = qkv[:, 2 * d:].T
    r_all = vt.shape[1]
    vt = jnp.concatenate([vt.reshape(heads, hd, r_all), jnp.ones((heads, 16, r_all), BF16)], axis=1)
    vt = vt.reshape(heads * (hd + 16), r_all)
    ctx_block = lay.batch * lay.seq // lay.ctx
    sub = jnp.broadcast_to(subln.reshape(hd, 1), (hd, LANES))
    smem = pl.BlockSpec(memory_space=pltpu.SMEM)
    ctx_specs = [pl.BlockSpec((lay.ctx, hd), lambda b, h, qi: (ctx_block + b, heads + h)),
                 pl.BlockSpec((hd + 16, lay.ctx), lambda b, h, qi: (h, ctx_block + b))]
    sub_spec = pl.BlockSpec((hd, LANES), lambda b, h, qi: (0, 0))

    def scratch(tq, tk):
        return ([pltpu.VMEM((hd, 2 * tq), BF16), pltpu.VMEM((2, tk, 2 * tq), F32)]
                + [pltpu.VMEM((1, 2 * tq), F32)] * 3 + [pltpu.VMEM((hd + 16, 2 * tq), F32)])

    tq = _pick(lay.seq, (256, 128))
    tk = _pick(lay.seq, (512, 256, 128))
    q_tiles = _pick(lay.seq // tq, (4, 2, 1))
    n_q = lay.seq // (tq * q_tiles)
    a_lat = pl.pallas_call(
        functools.partial(_attn_kernel, latents=True, tk=tk, q_tiles=q_tiles, out_scale=1.0 - lam_init),
        grid=(lay.batch, heads, n_q),
        in_specs=[smem, pl.BlockSpec((tq * q_tiles, hd), lambda b, h, qi: (b * n_q + qi, h))] + ctx_specs
        + [pl.BlockSpec((lay.seq, hd), lambda b, h, qi: (b, heads + h)),
           pl.BlockSpec((hd + 16, lay.seq), lambda b, h, qi: (h, b)), sub_spec],
        out_specs=pl.BlockSpec((tq * q_tiles, hd), lambda b, h, qi: (b * n_q + qi, h)),
        out_shape=jax.ShapeDtypeStruct((lay.batch * lay.seq, d), BF16),
        scratch_shapes=scratch(tq, tk),
        compiler_params=_cparams(("parallel", "parallel", "arbitrary")),
    )(lam, qkv, qkv, vt, qkv, vt, sub)
    if not need_ctx:
        return a_lat, None
    a_ctx = pl.pallas_call(
        functools.partial(_attn_kernel, latents=False, tk=SUBLANES, q_tiles=1, out_scale=1.0 - lam_init),
        grid=(lay.batch, heads, 1),
        in_specs=[smem, pl.BlockSpec((lay.ctx, hd), lambda b, h, qi: (ctx_block + b, h))] + ctx_specs + [sub_spec],
        out_specs=pl.BlockSpec((lay.ctx, hd), lambda b, h, qi: (b, h)),
        out_shape=jax.ShapeDtypeStruct((lay.batch * lay.ctx, d), BF16),
        scratch_shapes=scratch(lay.ctx, SUBLANES),
        compiler_params=_cparams(("parallel", "parallel", "arbitrary")),
    )(lam, qkv, qkv, vt, sub)
    return a_lat, a_ctx


def _attnout_kernel(x_ref, mod_ref, g_ref, a_ref, *rest, d, n_lat_tiles):
    _, _, gate = _mod3(mod_ref, 1, d)
    if len(rest) == 3:
        actx_ref, w_ref, o_ref = rest
        a = jnp.where(pl.program_id(0) < n_lat_tiles, a_ref[...], actx_ref[...])
    else:
        w_ref, o_ref = rest
        a = a_ref[...]
    y = _dot(a, w_ref[...])
    o_ref[...] = x_ref[...] + gate * _rms(y, g_ref[3:4, :])


def _attnout(xt, mod, norm_g, a_lat, a_ctx, w_o, layer, i_odd, lay):
    d = xt.shape[1]
    tm = min(lay.tm, 256)
    lay_m = lay.with_tm(tm)
    n_lat = lay_m.n_lat_tiles
    r = a_lat.shape[0] + (0 if a_ctx is None else a_ctx.shape[0])
    a_specs = [pl.BlockSpec((tm, d), lambda i: (jnp.minimum(i, n_lat - 1), 0))]
    a_args = [a_lat]
    if a_ctx is not None:
        a_specs.append(pl.BlockSpec((tm, d), lambda i: (jnp.maximum(i - n_lat, 0), 0)))
        a_args.append(a_ctx)
    return pl.pallas_call(
        functools.partial(_attnout_kernel, d=d, n_lat_tiles=n_lat),
        grid=(r // tm,),
        in_specs=[pl.BlockSpec((tm, d), lambda i: (i, 0)),
                  pl.BlockSpec((None, None, 1, N_MOD * d), lambda i: (layer, lay_m.mod_row(i), 0, 0)),
                  pl.BlockSpec((None, 6, d), lambda i: (layer, 0, 0))] + a_specs
        + [pl.BlockSpec((None, d, d), lambda i: (i_odd, 0, 0))],
        out_specs=pl.BlockSpec((tm, d), lambda i: (i, 0)),
        out_shape=jax.ShapeDtypeStruct((r, d), F32),
        compiler_params=_cparams(("parallel",)),
    )(xt, mod, norm_g, *a_args, w_o)


class _Layout:
    def __init__(self, batch, seq, ctx, tm, q_scale):
        self.batch, self.seq, self.ctx, self.tm, self.q_scale = batch, seq, ctx, tm, q_scale
        self.tiles_per_seq = seq // tm
        self.n_lat_tiles = batch * self.tiles_per_seq

    def with_tm(self, tm):
        return _Layout(self.batch, self.seq, self.ctx, tm, self.q_scale)

    def mod_row(self, i):
        return jnp.minimum(i // self.tiles_per_seq, self.batch)


def kernel(x, c, ctx, c_ctx, w_mod, b_mod, norm_g, ffn_wg, ffn_wu, ffn_wd, mix_w_in, mix_w_out, s5_lam_re, s5_lam_im, s5_log_step, s5_b_re, s5_b_im, s5_c_re, s5_c_im, s5_d, s5_w_glu, s5_b_glu, conv_w, conv_b, attn_w_qkv, attn_w_o, attn_lambda, attn_subln):
    batch, seq, d = x.shape
    n_ctx = ctx.shape[1]
    depth = w_mod.shape[0]
    hd = attn_subln.shape[-1]
    dh = attn_lambda.shape[-1]
    g, p, h = s5_b_re.shape[2:]
    sw = g * h
    t = S5_CHUNK
    assert batch + 1 <= SUBLANES and conv_w.shape[-1] == sw and mix_w_in.shape[-1] == 4 * sw
    assert g % S5_BLOCK == 0 and (S5_BLOCK * p) % LANES == 0 and (S5_BLOCK * h * t) % LANES == 0
    assert seq % (t * SUBLANES) == 0 and n_ctx % (t * SUBLANES) == 0 and seq % n_ctx == 0 and hd == 2 * dh
    tm = _pick(math.gcd(seq, batch * n_ctx), (512, 256, 128, 64, 32, 16))
    lay = _Layout(batch, seq, n_ctx, tm, dh ** -0.5 * math.log2(math.e))
    r = batch * (seq + n_ctx)

    xt = jnp.concatenate([x.reshape(batch * seq, d), ctx.reshape(batch * n_ctx, d)], axis=0)
    cc = jnp.zeros((SUBLANES, d), F32).at[:batch].set(c).at[batch].set(c_ctx)
    mod = _ada(cc, w_mod, b_mod).reshape(depth, SUBLANES, 1, N_MOD * d)

    wg, wu, wd = ffn_wg.astype(BF16), ffn_wu.astype(BF16), ffn_wd.astype(BF16)
    w_in, w_out, w_glu = mix_w_in.astype(BF16), mix_w_out.astype(BF16), s5_w_glu.astype(BF16)
    w_qkv, w_o = attn_w_qkv.astype(BF16), attn_w_o.astype(BF16)

    cos, sin = _rope_tables(seq)
    pos = jnp.concatenate([jnp.tile(jnp.arange(seq), batch), jnp.tile(jnp.arange(n_ctx), batch)])
    last = jnp.concatenate([jnp.full((batch * seq,), seq - 1), jnp.full((batch * n_ctx,), n_ctx - 1)])
    keep_prev = (pos != 0).astype(F32)[:, None]
    keep_next = (pos != last).astype(F32)[:, None]
    nc = r // t
    mats = jax.vmap(functools.partial(_s5_matrices, t=t))(s5_lam_re, s5_lam_im, s5_log_step, s5_b_re, s5_b_im,
                                                          s5_c_re, s5_c_im)

    for layer in range(depth):
        i = layer // 2
        xt = _ffn(xt, mod, norm_g, wg, wu, wd, layer, 0, 0, lay)
        if layer % 2 == 0:
            pj = _proj(xt, mod, norm_g, w_in, i, layer, lay)
            nb, wb = g // S5_BLOCK, S5_BLOCK * h
            uf = pj[:, :sw].reshape(nc, t, nb, wb).transpose(2, 0, 1, 3).reshape(nb, nc, t * wb)
            ys = _s5(uf, mats, i, lay)
            ys = ys.reshape(nb, nc, t, wb).transpose(1, 2, 0, 3).reshape(r, sw)
            xt = _mixout(xt, mod, norm_g, ys, pj, keep_prev, keep_next, s5_d, w_glu, s5_b_glu, conv_w, conv_b, w_out,
                         layer, i, lay)
        else:
            lam_init = 0.8 - 0.6 * math.exp(-0.3 * layer)
            lv = attn_lambda[i].astype(F32)
            lam = (jnp.exp(jnp.sum(lv[0] * lv[1])) - jnp.exp(jnp.sum(lv[2] * lv[3])) + lam_init).reshape(1)
            qkv = _proj(xt, mod, norm_g, w_qkv, i, layer, lay, rope=(cos, sin, 2 * d, d))
            a_lat, a_ctx = _attn(qkv, lam, attn_subln[i], lam_init, lay, need_ctx=layer < depth - 1)
            xt = _attnout(xt, mod, norm_g, a_lat, a_ctx, w_o, layer, i, lay)
        xt = _ffn(xt, mod, norm_g, wg, wu, wd, layer, 2, 1, lay, rows=batch * seq if layer == depth - 1 else None)
    return xt.reshape(batch, seq, d)
```
